```python
import math, functools
import jax, jax.numpy as jnp
from jax import lax
import numpy as np

D_MODEL = 1024
BATCH = 2
SEQ = 8192
DEPTH = 2
DEC_BATCH = 128
DEC_SEQ = 1
PAST_LEN = 2048
PAGE_SIZE = 128

BRANCH_W = D_MODEL
N_BRANCH = 3
POOL_WINDOWS = (2, 4, 8, 16)
POOL_GROUPS = len(POOL_WINDOWS)
POOL_GW = BRANCH_W // POOL_GROUPS
POOL_BUF = max(POOL_WINDOWS) - 1
S5_GROUP = 16
S5_GROUPS = BRANCH_W // S5_GROUP
S5_STATE = 64
DT_MIN = 0.001
DT_MAX = 0.1
SB_HEADS = 16
SB_HEAD_DIM = BRANCH_W // SB_HEADS
SB_SCALE = 1.0 / math.sqrt(SB_HEAD_DIM)
SB_BIAS_INIT = -6.0
Q_BLOCK = 128
IN_SIZES = (BRANCH_W,) * 8 + (D_MODEL,) * N_BRANCH
IN_SPLITS = tuple(int(s) for s in np.cumsum(IN_SIZES)[:-1])
N_IN = int(sum(IN_SIZES))
ALPHA = (2.0 * DEPTH) ** 0.25
BETA_INIT = (8.0 * DEPTH) ** -0.25
LN_EPS = 1e-5

kernel_name = "gated_pool_s5_stickbreak_decoder_step"


def layer_norm(x, g, b):
    xf = x.astype(jnp.float32)
    mu = jnp.mean(xf, axis=-1, keepdims=True)
    var = jnp.mean(jnp.square(xf - mu), axis=-1, keepdims=True)
    return ((xf - mu) * lax.rsqrt(var + LN_EPS) * g + b).astype(x.dtype)


def pool_branch(u, prev, pos0, pool_w, pool_scale):
    bsz, t_len, _ = u.shape
    uf = u.astype(jnp.float32)
    u_ext = jnp.concatenate([prev.astype(jnp.float32), uf], axis=1)
    cs = jnp.pad(jnp.cumsum(u_ext, axis=1), ((0, 0), (1, 0), (0, 0)))
    pos = pos0 + jnp.arange(t_len)
    groups = []
    for g, w in enumerate(POOL_WINDOWS):
        sl = slice(g * POOL_GW, (g + 1) * POOL_GW)
        win = cs[:, POOL_BUF + 1:POOL_BUF + 1 + t_len, sl] - cs[:, POOL_BUF + 1 - w:POOL_BUF + 1 - w + t_len, sl]
        cnt = jnp.minimum(pos + 1, w).astype(jnp.float32)
        groups.append(win / cnt[None, :, None] - uf[..., sl])
    pooled = jnp.stack(groups, axis=2)
    mixed = jnp.einsum('btgc,gce->btge', pooled, pool_w.astype(jnp.float32))
    y = mixed.reshape(bsz, t_len, BRANCH_W) * pool_scale.astype(jnp.float32)
    return y, u_ext[:, -POOL_BUF:].astype(u.dtype)


def _lin_comb(e1, e2):
    a1, b1 = e1
    a2, b2 = e2
    return a1 * a2, a2 * b1 + b2


def s5_branch(u, prev_re, prev_im, a_re, a_im, b_re, b_im, c_re, c_im, d, log_dt, w_glu, b_glu):
    f32 = jnp.float32
    bsz, t_len, _ = u.shape
    lam = lax.complex(a_re.astype(f32), a_im.astype(f32))
    dt = jnp.exp(log_dt.astype(f32))[:, None]
    a_bar = jnp.exp(lam * dt)
    b_mat = lax.complex(b_re.astype(f32), b_im.astype(f32))
    b_bar = ((a_bar - 1.0) / lam)[..., None] * b_mat
    ug = u.astype(f32).reshape(bsz, t_len, S5_GROUPS, S5_GROUP)
    bu = jnp.einsum('btgc,gpc->btgp', ug.astype(jnp.complex64), b_bar)
    a_seq = jnp.broadcast_to(a_bar, (1, t_len) + a_bar.shape)
    a_cum, h = lax.associative_scan(_lin_comb, (a_seq, bu), axis=1)
    if prev_re is not None:
        h0 = lax.complex(prev_re.astype(f32), prev_im.astype(f32))
        h = h + a_cum * h0[:, None]
    c_mat = lax.complex(c_re.astype(f32), c_im.astype(f32))
    y = jnp.real(jnp.einsum('btgp,gcp->btgc', h, c_mat)) + d.astype(f32).reshape(S5_GROUPS, S5_GROUP) * ug
    y = jax.nn.gelu(y.reshape(bsz, t_len, BRANCH_W))
    y = y * jax.nn.sigmoid(y @ w_glu.astype(f32) + b_glu.astype(f32))
    h_last = h[:, -1]
    return y, jnp.real(h_last), jnp.imag(h_last)


def stick_breaking(q, k, v, bias, q_pos, k_pos):
    z = jnp.einsum('bqhd,bkhd->bhqk', q.astype(jnp.float32), k) * SB_SCALE
    z = z + bias.astype(jnp.float32)[None, :, None, None]
    causal = (k_pos[None, :] < q_pos[:, None])[None, None]
    log_keep = jnp.where(causal, jax.nn.log_sigmoid(-z), 0.0)
    between = lax.cumsum(log_keep, axis=3, reverse=True) - log_keep
    w = jnp.where(causal, jnp.exp(jax.nn.log_sigmoid(z) + between), 0.0)
    return jnp.einsum('bhqk,bkhd->bqhd', w, v)


def prompt_attend(q, k, v, bias):
    bsz, t_len = q.shape[:2]
    nb = t_len // Q_BLOCK
    qb = q.reshape(bsz, nb, Q_BLOCK, SB_HEADS, SB_HEAD_DIM).transpose(1, 0, 2, 3, 4)
    pos = jnp.arange(t_len)
    pb = pos.reshape(nb, Q_BLOCK)
    kf = k.astype(jnp.float32)
    vf = v.astype(jnp.float32)
    out = lax.map(lambda a: stick_breaking(a[0], kf, vf, bias, a[1], pos), (qb, pb))
    return out.transpose(1, 0, 2, 3, 4).reshape(bsz, t_len, SB_HEADS, SB_HEAD_DIM)


def sample_attend(q, k, v, bias, k_past, v_past, past_len):
    t_len = q.shape[1]
    k_all = jnp.concatenate([k_past.astype(jnp.float32), k.astype(jnp.float32)], axis=1)
    v_all = jnp.concatenate([v_past.astype(jnp.float32), v.astype(jnp.float32)], axis=1)
    q_pos = past_len + jnp.arange(t_len)
    k_pos = jnp.arange(past_len + t_len)
    return stick_breaking(q, k_all, v_all, bias, q_pos, k_pos)


def trunk_layer(x, p, pool_prev, s5_prev_re, s5_prev_im, attend, pos0):
    bsz, t_len, _ = x.shape
    proj = x @ p['w_in']
    u_a, z_a, u_b, z_b, q, k, v, z_c, g_a, g_b, g_c = jnp.split(proj, IN_SPLITS, axis=-1)
    y_a, pool_new = pool_branch(u_a, pool_prev, pos0, p['pool_w'], p['pool_scale'])
    y_b, s5_re, s5_im = s5_branch(u_b, s5_prev_re, s5_prev_im, p['s5_a_re'], p['s5_a_im'],
                                  p['s5_b_re'], p['s5_b_im'], p['s5_c_re'], p['s5_c_im'],
                                  p['s5_d'], p['s5_log_dt'], p['w_glu'], p['b_glu'])
    kh = k.reshape(bsz, t_len, SB_HEADS, SB_HEAD_DIM)
    vh = v.reshape(bsz, t_len, SB_HEADS, SB_HEAD_DIM)
    qh = q.reshape(bsz, t_len, SB_HEADS, SB_HEAD_DIM)
    y_c = attend(qh, kh, vh, p['sb_bias']).reshape(bsz, t_len, BRANCH_W)
    silu = jax.nn.silu
    branches = jnp.stack([y_a * silu(z_a.astype(jnp.float32)),
                          y_b * silu(z_b.astype(jnp.float32)),
                          y_c * silu(z_c.astype(jnp.float32))], axis=2)
    per_branch = jnp.einsum('btnw,nwd->btnd', branches, p['w_branch'].astype(jnp.float32))
    gates = jax.nn.sigmoid(jnp.stack([g_a, g_b, g_c], axis=2).astype(jnp.float32))
    merged = jnp.sum(gates * per_branch, axis=2)
    out = merged @ p['w_out'].astype(jnp.float32)
    x_new = layer_norm(ALPHA * x.astype(jnp.float32) + out, p['ln_g'], p['ln_b']).astype(x.dtype)
    return x_new, (kh, vh, pool_new, s5_re, s5_im)


def setup_inputs(seed: int = 0) -> dict:
    key = jax.random.key(seed)
    ks = jax.random.split(key, 27)
    f32 = jnp.float32
    n_pages = PAST_LEN // PAGE_SIZE
    n_used = DEC_BATCH * n_pages
    n_pool = (5 * n_used + 3) // 4
    nrm = lambda k, shape, s: (jax.random.normal(k, shape, f32) * s).astype(f32)
    page_table = jax.random.permutation(ks[0], n_pool)[:n_used].reshape(DEC_BATCH, n_pages).astype(jnp.int32)
    a_im = jnp.broadcast_to(jnp.pi * jnp.arange(S5_STATE, dtype=f32), (DEPTH, S5_GROUPS, S5_STATE))
    return {
        'x_prompt': nrm(ks[1], (BATCH, SEQ, D_MODEL), 1.0),
        'x_sample': nrm(ks[2], (DEC_BATCH, DEC_SEQ, D_MODEL), 1.0),
        'cache_k': nrm(ks[3], (DEPTH, n_pool, PAGE_SIZE, SB_HEADS, SB_HEAD_DIM), 1.0),
        'cache_v': nrm(ks[4], (DEPTH, n_pool, PAGE_SIZE, SB_HEADS, SB_HEAD_DIM), 1.0),
        'state_pool': nrm(ks[5], (DEPTH, DEC_BATCH, POOL_BUF, BRANCH_W), 1.0),
        'state_s5_re': nrm(ks[6], (DEPTH, DEC_BATCH, S5_GROUPS, S5_STATE), 0.3),
        'state_s5_im': nrm(ks[7], (DEPTH, DEC_BATCH, S5_GROUPS, S5_STATE), 0.3),
        'page_table': page_table,
        'w_in': nrm(ks[8], (DEPTH, D_MODEL, N_IN), D_MODEL ** -0.5),
        'pool_w': nrm(ks[9], (DEPTH, POOL_GROUPS, POOL_GW, POOL_GW), POOL_GW ** -0.5),
        'pool_scale': 1.0 + nrm(ks[10], (DEPTH, BRANCH_W), 0.02),
        's5_a_re': -0.5 + nrm(ks[11], (DEPTH, S5_GROUPS, S5_STATE), 0.01),
        's5_a_im': a_im + nrm(ks[12], (DEPTH, S5_GROUPS, S5_STATE), 0.01),
        's5_b_re': nrm(ks[13], (DEPTH, S5_GROUPS, S5_STATE, S5_GROUP), (2.0 * S5_GROUP) ** -0.5),
        's5_b_im': nrm(ks[14], (DEPTH, S5_GROUPS, S5_STATE, S5_GROUP), (2.0 * S5_GROUP) ** -0.5),
        's5_c_re': nrm(ks[15], (DEPTH, S5_GROUPS, S5_GROUP, S5_STATE), (2.0 * S5_STATE) ** -0.5),
        's5_c_im': nrm(ks[16], (DEPTH, S5_GROUPS, S5_GROUP, S5_STATE), (2.0 * S5_STATE) ** -0.5),
        's5_d': nrm(ks[17], (DEPTH, BRANCH_W), 1.0),
        's5_log_dt': jax.random.uniform(ks[18], (DEPTH, S5_GROUPS), f32, math.log(DT_MIN), math.log(DT_MAX)),
        'w_glu': nrm(ks[19], (DEPTH, BRANCH_W, BRANCH_W), BRANCH_W ** -0.5),
        'b_glu': nrm(ks[20], (DEPTH, BRANCH_W), 0.02),
        'sb_bias': SB_BIAS_INIT + nrm(ks[25], (DEPTH, SB_HEADS), 0.1),
        'w_branch': nrm(ks[21], (DEPTH, N_BRANCH, BRANCH_W, D_MODEL), BETA_INIT * BRANCH_W ** -0.5),
        'w_out': nrm(ks[22], (DEPTH, D_MODEL, D_MODEL), BETA_INIT * D_MODEL ** -0.5),
        'ln_g': 1.0 + nrm(ks[23], (DEPTH, D_MODEL), 0.02),
        'ln_b': nrm(ks[24], (DEPTH, D_MODEL), 0.02),
    }


def reference(x_prompt, x_sample, cache_k, cache_v, state_pool, state_s5_re, state_s5_im, page_table,
              w_in, pool_w, pool_scale, s5_a_re, s5_a_im, s5_b_re, s5_b_im, s5_c_re, s5_c_im,
              s5_d, s5_log_dt, w_glu, b_glu, sb_bias, w_branch, w_out, ln_g, ln_b):
    dec_b = x_sample.shape[0]
    page_size = cache_k.shape[2]
    past_len = page_table.shape[1] * page_size
    y_prompt, y_sample = x_prompt, x_sample
    kp, vp, poolp, s5rp, s5ip = [], [], [], [], []
    ks_, vs_, pools, s5rs, s5is = [], [], [], [], []
    for l in range(DEPTH):
        p = dict(w_in=w_in[l], pool_w=pool_w[l], pool_scale=pool_scale[l],
                 s5_a_re=s5_a_re[l], s5_a_im=s5_a_im[l], s5_b_re=s5_b_re[l], s5_b_im=s5_b_im[l],
                 s5_c_re=s5_c_re[l], s5_c_im=s5_c_im[l], s5_d=s5_d[l], s5_log_dt=s5_log_dt[l],
                 w_glu=w_glu[l], b_glu=b_glu[l], sb_bias=sb_bias[l], w_branch=w_branch[l],
                 w_out=w_out[l], ln_g=ln_g[l], ln_b=ln_b[l])
        pool0 = jnp.zeros((x_prompt.shape[0], POOL_BUF, BRANCH_W), x_prompt.dtype)
        y_prompt, (k_n, v_n, pool_n, s5r_n, s5i_n) = trunk_layer(y_prompt, p, pool0, None, None, prompt_attend, 0)
        kp.append(k_n); vp.append(v_n); poolp.append(pool_n); s5rp.append(s5r_n); s5ip.append(s5i_n)
        k_past = cache_k[l][page_table].reshape(dec_b, past_len, SB_HEADS, SB_HEAD_DIM)
        v_past = cache_v[l][page_table].reshape(dec_b, past_len, SB_HEADS, SB_HEAD_DIM)
        attend = functools.partial(sample_attend, k_past=k_past, v_past=v_past, past_len=past_len)
        y_sample, (k_n, v_n, pool_n, s5r_n, s5i_n) = trunk_layer(
            y_sample, p, state_pool[l], state_s5_re[l], state_s5_im[l], attend, past_len)
        ks_.append(k_n); vs_.append(v_n); pools.append(pool_n); s5rs.append(s5r_n); s5is.append(s5i_n)
    return (y_prompt, y_sample,
            jnp.stack(kp), jnp.stack(vp), jnp.stack(poolp), jnp.stack(s5rp), jnp.stack(s5ip),
            jnp.stack(ks_), jnp.stack(vs_), jnp.stack(pools), jnp.stack(s5rs), jnp.stack(s5is))
```

```python
import functools
import math

import jax
import jax.numpy as jnp
from jax import lax
from jax.experimental import pallas as pl
from jax.experimental.pallas import tpu as pltpu

F32 = jnp.float32
BF16 = jnp.bfloat16
HIGHEST = lax.Precision.HIGHEST

D_MODEL = 1024
BRANCH_W = 1024
N_PROJ = 11
POOL_WINDOWS = (2, 4, 8, 16)
POOL_GW = BRANCH_W // len(POOL_WINDOWS)
POOL_BUF = max(POOL_WINDOWS) - 1
POOL_HALO = 16
S5_GROUP = 16
S5_GROUPS = BRANCH_W // S5_GROUP
S5_STATE = 64
S5_CHUNK = 16
S5_PAIRS = S5_GROUPS // 2
S5_PAIR_W = 2 * S5_GROUP
SB_HEADS = 16
SB_HEAD_DIM = BRANCH_W // SB_HEADS
SB_SCALE = 1.0 / math.sqrt(SB_HEAD_DIM)
HEAD_PAIRS = SB_HEADS // 2
LN_EPS = 1e-5
VMEM_LIMIT = 56 * 1024 * 1024


def _cparams(*sem):
    return pltpu.CompilerParams(dimension_semantics=sem, vmem_limit_bytes=VMEM_LIMIT)


def _silu(z):
    return z * jax.nn.sigmoid(z)


def _softplus(z):
    return jnp.maximum(z, 0.0) + jnp.log1p(jnp.exp(-jnp.abs(z)))


def _proj_kernel(x_ref, w_ref, o_ref, xb_ref):
    @pl.when(pl.program_id(1) == 0)
    def _():
        xb_ref[...] = x_ref[...].astype(BF16)

    o_ref[...] = jnp.dot(xb_ref[...], w_ref[...], preferred_element_type=F32)


def _project(x2d, w_bf16, tm):
    n = x2d.shape[0]
    return pl.pallas_call(
        _proj_kernel,
        grid=(n // tm, N_PROJ),
        in_specs=[pl.BlockSpec((tm, D_MODEL), lambda i, j: (i, 0)),
                  pl.BlockSpec((D_MODEL, BRANCH_W), lambda i, j: (0, j))],
        out_specs=pl.BlockSpec((None, tm, BRANCH_W), lambda i, j: (j, i, 0)),
        out_shape=jax.ShapeDtypeStruct((N_PROJ, n, BRANCH_W), F32),
        scratch_shapes=[pltpu.VMEM((tm, D_MODEL), BF16)],
        compiler_params=_cparams("parallel", "arbitrary"),
    )(x2d, w_bf16)


def _pool_mix(acc, cnt, u, z, w_ref, sc_ref, g):
    sl = slice(g * POOL_GW, (g + 1) * POOL_GW)
    pooled = acc / cnt - u
    mixed = jnp.dot(pooled.astype(BF16), w_ref[g], preferred_element_type=F32)
    return mixed * sc_ref[:, sl] * _silu(z)


def _pool_prompt_kernel(u_ref, halo_ref, z_ref, w_ref, sc_ref, o_ref, ext_ref, *, tm):
    i = pl.program_id(1)
    ext_ref[0:POOL_HALO, :] = jnp.where(i == 0, 0.0, halo_ref[...])
    ext_ref[POOL_HALO:POOL_HALO + tm, :] = u_ref[...]
    pos = i * tm + lax.broadcasted_iota(jnp.int32, (tm, 1), 0)
    for g, w in enumerate(POOL_WINDOWS):
        sl = slice(g * POOL_GW, (g + 1) * POOL_GW)
        acc = ext_ref[POOL_HALO:POOL_HALO + tm, sl]
        for k in range(1, w):
            acc = acc + ext_ref[POOL_HALO - k:POOL_HALO - k + tm, sl]
        cnt = jnp.minimum(pos + 1, w).astype(F32)
        o_ref[:, sl] = _pool_mix(acc, cnt, u_ref[:, sl], z_ref[:, sl], w_ref, sc_ref, g)


def _pool_prompt(proj4, pool_w_bf16, pool_scale, tm):
    _, b, t, _ = proj4.shape
    hb = tm // POOL_HALO
    return pl.pallas_call(
        functools.partial(_pool_prompt_kernel, tm=tm),
        grid=(b, t // tm),
        in_specs=[pl.BlockSpec((None, None, tm, BRANCH_W), lambda bb, i: (0, bb, i, 0)),
                  pl.BlockSpec((None, None, POOL_HALO, BRANCH_W),
                               lambda bb, i: (0, bb, jnp.maximum(i * hb - 1, 0), 0)),
                  pl.BlockSpec((None, None, tm, BRANCH_W), lambda bb, i: (1, bb, i, 0)),
                  pl.BlockSpec((len(POOL_WINDOWS), POOL_GW, POOL_GW), lambda bb, i: (0, 0, 0)),
                  pl.BlockSpec((1, BRANCH_W), lambda bb, i: (0, 0))],
        out_specs=pl.BlockSpec((None, tm, BRANCH_W), lambda bb, i: (bb, i, 0)),
        out_shape=jax.ShapeDtypeStruct((b, t, BRANCH_W), F32),
        scratch_shapes=[pltpu.VMEM((POOL_HALO + tm, BRANCH_W), F32)],
        compiler_params=_cparams("parallel", "parallel"),
    )(proj4, proj4, proj4, pool_w_bf16, pool_scale)


def _pool_sample_kernel(u_ref, prev_ref, z_ref, w_ref, sc_ref, o_ref, *, pos0):
    for g, w in enumerate(POOL_WINDOWS):
        sl = slice(g * POOL_GW, (g + 1) * POOL_GW)
        acc = u_ref[:, sl]
        for k in range(1, w):
            acc = acc + prev_ref[POOL_BUF - k, :, sl]
        cnt = float(min(pos0 + 1, w))
        o_ref[:, sl] = _pool_mix(acc, cnt, u_ref[:, sl], z_ref[:, sl], w_ref, sc_ref, g)


def _pool_sample(proj_s, prev_t, pool_w_bf16, pool_scale, pos0):
    s = proj_s.shape[1]
    return pl.pallas_call(
        functools.partial(_pool_sample_kernel, pos0=pos0),
        grid=(1,),
        in_specs=[pl.BlockSpec((None, s, BRANCH_W), lambda i: (0, 0, 0)),
                  pl.BlockSpec((POOL_BUF, s, BRANCH_W), lambda i: (0, 0, 0)),
                  pl.BlockSpec((None, s, BRANCH_W), lambda i: (1, 0, 0)),
                  pl.BlockSpec((len(POOL_WINDOWS), POOL_GW, POOL_GW), lambda i: (0, 0, 0)),
                  pl.BlockSpec((1, BRANCH_W), lambda i: (0, 0))],
        out_specs=pl.BlockSpec((s, BRANCH_W), lambda i: (0, 0)),
        out_shape=jax.ShapeDtypeStruct((s, BRANCH_W), F32),
        compiler_params=_cparams("arbitrary"),
    )(proj_s, prev_t, proj_s, pool_w_bf16, pool_scale)


def _s5_discretize(a_re, a_im, b_re, b_im, c_re, c_im, log_dt):
    lam = lax.complex(a_re, a_im)
    step = lam * jnp.exp(log_dt)[:, None]
    a_bar = jnp.exp(step)
    b_bar = ((a_bar - 1.0) / lam)[..., None] * lax.complex(b_re, b_im)
    c_mat = lax.complex(c_re, c_im)
    return step, a_bar, b_bar, c_mat


def _pair_blockdiag(m, rows_per_group, cols_per_group):
    g, big_r, r, big_c, c = m.shape
    out = jnp.zeros((g // 2, big_r, 2, r, big_c, 2, c), m.dtype)
    out = out.at[:, :, 0, :, :, 0, :].set(m[0::2])
    out = out.at[:, :, 1, :, :, 1, :].set(m[1::2])
    return out.reshape(g // 2, big_r * 2 * r, big_c * 2 * c)


def _s5_chunk_operators(a_re, a_im, b_re, b_im, c_re, c_im, log_dt):
    ell = S5_CHUNK
    step, _, b_bar, c_mat = _s5_discretize(a_re, a_im, b_re, b_im, c_re, c_im, log_dt)
    apow = jnp.exp(step[None] * jnp.arange(ell + 1, dtype=F32)[:, None, None])
    ca = c_mat[None] * apow[:, :, None, :]
    ca_re, ca_im = jnp.real(ca), jnp.imag(ca)
    bb_re, bb_im = jnp.real(b_bar), jnp.imag(b_bar)
    kmat = (jnp.einsum('jgop,gpi->gjoi', ca_re[:ell], bb_re, precision=HIGHEST)
            - jnp.einsum('jgop,gpi->gjoi', ca_im[:ell], bb_im, precision=HIGHEST))
    s_idx = jnp.arange(ell)[:, None]
    t_idx = jnp.arange(ell)[None, :]
    lag = jnp.clip(t_idx - s_idx, 0, ell - 1)
    tfull = jnp.where((t_idx >= s_idx)[None, :, :, None, None], kmat[:, lag], 0.0)
    tfull = tfull.transpose(0, 1, 4, 2, 3)
    t_op = _pair_blockdiag(tfull, S5_GROUP, S5_GROUP)
    w_c = apow[ell - 1 - jnp.arange(ell)][:, :, :, None] * b_bar[None]
    w_c = w_c.transpose(1, 0, 3, 2)
    w_re = _pair_blockdiag(jnp.real(w_c)[:, :, :, None, :], S5_GROUP, S5_STATE)
    w_im = _pair_blockdiag(jnp.imag(w_c)[:, :, :, None, :], S5_GROUP, S5_STATE)
    w_op = jnp.concatenate([w_re, w_im], axis=2)
    v_re = ca_re[1:].transpose(1, 3, 0, 2)
    v_im = -ca_im[1:].transpose(1, 3, 0, 2)
    v_re = _pair_blockdiag(v_re[:, None], S5_STATE, S5_GROUP)
    v_im = _pair_blockdiag(v_im[:, None], S5_STATE, S5_GROUP)
    v_op = jnp.concatenate([v_re, v_im], axis=1)
    a_l = apow[ell]
    a_op = jnp.concatenate([jnp.real(a_l).reshape(S5_PAIRS, 2 * S5_STATE),
                            jnp.imag(a_l).reshape(S5_PAIRS, 2 * S5_STATE)], axis=1)
    return t_op.astype(BF16), w_op.astype(BF16), v_op.astype(BF16), a_op


def _s5_state_kernel(u_ref, w_ref, o_ref):
    o_ref[...] = jnp.dot(u_ref[...], w_ref[...], preferred_element_type=F32)


def _s5_chunk_states(u2, w_op):
    gp, r, kw = u2.shape
    nw = w_op.shape[2]
    return pl.pallas_call(
        _s5_state_kernel,
        grid=(gp,),
        in_specs=[pl.BlockSpec((None, r, kw), lambda g: (g, 0, 0)),
                  pl.BlockSpec((None, kw, nw), lambda g: (g, 0, 0))],
        out_specs=pl.BlockSpec((None, r, nw), lambda g: (g, 0, 0)),
        out_shape=jax.ShapeDtypeStruct((gp, r, nw), F32),
        compiler_params=_cparams("parallel"),
    )(u2, w_op)


def _s5_scan_kernel(s_ref, a_ref, hprev_ref, hlast_ref, h_scr, *, cb):
    j = pl.program_id(1)
    half = 2 * S5_STATE

    @pl.when(j == 0)
    def _():
        h_scr[...] = jnp.zeros_like(h_scr)

    a_r = a_ref[:, :half]
    a_i = a_ref[:, half:]

    def body(c, carry):
        h_r, h_i = carry
        hprev_ref[c, :, :half] = h_r.astype(BF16)
        hprev_ref[c, :, half:] = h_i.astype(BF16)
        s = s_ref[c]
        n_r = a_r * h_r - a_i * h_i + s[:, :half]
        n_i = a_r * h_i + a_i * h_r + s[:, half:]
        return n_r, n_i

    h_r, h_i = lax.fori_loop(0, cb, body, (h_scr[:, :half], h_scr[:, half:]))
    h_scr[:, :half] = h_r
    h_scr[:, half:] = h_i

    @pl.when(j == pl.num_programs(1) - 1)
    def _():
        hlast_ref[...] = h_scr[...]


def _s5_scan(s_t, a_op, cb):
    b, nc, gp, w = s_t.shape
    return pl.pallas_call(
        functools.partial(_s5_scan_kernel, cb=cb),
        grid=(b, nc // cb),
        in_specs=[pl.BlockSpec((None, cb, gp, w), lambda bb, j: (bb, j, 0, 0)),
                  pl.BlockSpec((gp, w), lambda bb, j: (0, 0))],
        out_specs=[pl.BlockSpec((None, cb, gp, w), lambda bb, j: (bb, j, 0, 0)),
                   pl.BlockSpec((None, gp, w), lambda bb, j: (bb, 0, 0))],
        out_shape=[jax.ShapeDtypeStruct((b, nc, gp, w), BF16),
                   jax.ShapeDtypeStruct((b, gp, w), F32)],
        scratch_shapes=[pltpu.VMEM((gp, w), F32)],
        compiler_params=_cparams("parallel", "arbitrary"),
    )(s_t, a_op)


def _s5_out_kernel(u_ref, h_ref, t_ref, v_ref, o_ref):
    o_ref[...] = (jnp.dot(u_ref[...], t_ref[...], preferred_element_type=F32)
                  + jnp.dot(h_ref[...], v_ref[...], preferred_element_type=F32))


def _s5_chunk_outputs(u2, h2, t_op, v_op):
    gp, r, kw = u2.shape
    hw = h2.shape[2]
    return pl.pallas_call(
        _s5_out_kernel,
        grid=(gp,),
        in_specs=[pl.BlockSpec((None, r, kw), lambda g: (g, 0, 0)),
                  pl.BlockSpec((None, r, hw), lambda g: (g, 0, 0)),
                  pl.BlockSpec((None, kw, kw), lambda g: (g, 0, 0)),
                  pl.BlockSpec((None, hw, kw), lambda g: (g, 0, 0))],
        out_specs=pl.BlockSpec((None, r, kw), lambda g: (g, 0, 0)),
        out_shape=jax.ShapeDtypeStruct((gp, r, kw), F32),
        compiler_params=_cparams("parallel"),
    )(u2, h2, t_op, v_op)


def _s5_prompt(u_b, ops):
    t_op, w_op, v_op, a_op = ops
    b, t, _ = u_b.shape
    nc = t // S5_CHUNK
    u2 = u_b.reshape(b, nc, S5_CHUNK, S5_PAIRS, S5_PAIR_W).transpose(3, 0, 1, 2, 4)
    u2 = u2.reshape(S5_PAIRS, b * nc, S5_CHUNK * S5_PAIR_W).astype(BF16)
    s_loc = _s5_chunk_states(u2, w_op)
    s_t = s_loc.reshape(S5_PAIRS, b, nc, -1).transpose(1, 2, 0, 3)
    h_prev, h_last = _s5_scan(s_t, a_op, min(nc, 128))
    h2 = h_prev.transpose(2, 0, 1, 3).reshape(S5_PAIRS, b * nc, -1)
    y2 = _s5_chunk_outputs(u2, h2, t_op, v_op)
    y = y2.reshape(S5_PAIRS, b, nc, S5_CHUNK, S5_PAIR_W).transpose(1, 2, 3, 0, 4)
    return y.reshape(b, t, BRANCH_W), h_last


def _s5_sample_operators(a_re, a_im, b_re, b_im, c_re, c_im, log_dt):
    _, a_bar, b_bar, c_mat = _s5_discretize(a_re, a_im, b_re, b_im, c_re, c_im, log_dt)
    ns = BRANCH_W // 256
    gs = S5_GROUPS // ns
    eye = jnp.eye(gs, dtype=F32)

    def expand_b(m):
        m = m.reshape(ns, gs, S5_STATE, S5_GROUP).transpose(0, 1, 3, 2)
        return jnp.einsum('ngcp,gh->ngchp', m, eye).reshape(ns, gs * S5_GROUP, gs * S5_STATE)

    def expand_c(m):
        m = m.reshape(ns, gs, S5_GROUP, S5_STATE).transpose(0, 1, 3, 2)
        return jnp.einsum('ngpc,gh->ngphc', m, eye).reshape(ns, gs * S5_STATE, gs * S5_GROUP)

    b_op = jnp.concatenate([expand_b(jnp.real(b_bar)), expand_b(jnp.imag(b_bar))], axis=2)
    c_op = jnp.concatenate([expand_c(jnp.real(c_mat)), expand_c(-jnp.imag(c_mat))], axis=1)
    a_r = jnp.real(a_bar).reshape(1, S5_GROUPS * S5_STATE)
    a_i = jnp.imag(a_bar).reshape(1, S5_GROUPS * S5_STATE)
    return b_op.astype(BF16), c_op.astype(BF16), a_r, a_i


def _s5_sample_kernel(u_ref, hr_ref, hi_ref, ar_ref, ai_ref, b_ref, c_ref, y_ref, or_ref, oi_ref):
    ns = b_ref.shape[0]
    cw = b_ref.shape[1]
    sw = c_ref.shape[1] // 2
    for n in range(ns):
        cs = slice(n * cw, (n + 1) * cw)
        ss = slice(n * sw, (n + 1) * sw)
        bu = jnp.dot(u_ref[:, cs].astype(BF16), b_ref[n], preferred_element_type=F32)
        a_r, a_i = ar_ref[:, ss], ai_ref[:, ss]
        h_r, h_i = hr_ref[:, ss], hi_ref[:, ss]
        n_r = a_r * h_r - a_i * h_i + bu[:, :sw]
        n_i = a_r * h_i + a_i * h_r + bu[:, sw:]
        or_ref[:, ss] = n_r
        oi_ref[:, ss] = n_i
        hcat = jnp.concatenate([n_r, n_i], axis=1).astype(BF16)
        y_ref[:, cs] = jnp.dot(hcat, c_ref[n], preferred_element_type=F32)


def _s5_sample(proj_s, h_re, h_im, ops):
    b_op, c_op, a_r, a_i = ops
    s = proj_s.shape[1]
    sw = h_re.shape[1]
    full = lambda shape: pl.BlockSpec(shape, lambda i: (0,) * len(shape))
    return pl.pallas_call(
        _s5_sample_kernel,
        grid=(1,),
        in_specs=[pl.BlockSpec((None, s, BRANCH_W), lambda i: (2, 0, 0)),
                  full((s, sw)), full((s, sw)), full((1, sw)), full((1, sw)),
                  full(b_op.shape), full(c_op.shape)],
        out_specs=[full((s, BRANCH_W)), full((s, sw)), full((s, sw))],
        out_shape=[jax.ShapeDtypeStruct((s, BRANCH_W), F32),
                   jax.ShapeDtypeStruct((s, sw), F32),
                   jax.ShapeDtypeStruct((s, sw), F32)],
        compiler_params=_cparams("arbitrary"),
    )(proj_s, h_re, h_im, a_r, a_i, b_op, c_op)


def _glu_kernel(y_ref, u_ref, z_ref, d_ref, w_ref, b_ref, o_ref):
    y = y_ref[...] + d_ref[...] * u_ref[...]
    g = jax.nn.gelu(y, approximate=True)
    t = jnp.dot(g.astype(BF16), w_ref[...], preferred_element_type=F32) + b_ref[...]
    o_ref[...] = g * jax.nn.sigmoid(t) * _silu(z_ref[...])


def _glu(y2d, proj, s5_d, w_glu_bf16, b_glu, tm):
    n = y2d.shape[0]
    return pl.pallas_call(
        _glu_kernel,
        grid=(n // tm,),
        in_specs=[pl.BlockSpec((tm, BRANCH_W), lambda i: (i, 0)),
                  pl.BlockSpec((None, tm, BRANCH_W), lambda i: (2, i, 0)),
                  pl.BlockSpec((None, tm, BRANCH_W), lambda i: (3, i, 0)),
                  pl.BlockSpec((1, BRANCH_W), lambda i: (0, 0)),
                  pl.BlockSpec((BRANCH_W, BRANCH_W), lambda i: (0, 0)),
                  pl.BlockSpec((1, BRANCH_W), lambda i: (0, 0))],
        out_specs=pl.BlockSpec((tm, BRANCH_W), lambda i: (i, 0)),
        out_shape=jax.ShapeDtypeStruct((n, BRANCH_W), F32),
        compiler_params=_cparams("parallel"),
    )(y2d, proj, proj, s5_d, w_glu_bf16, b_glu)


def _suffix_ones(n):
    r = lax.broadcasted_iota(jnp.int32, (n, n), 0)
    c = lax.broadcasted_iota(jnp.int32, (n, n), 1)
    return (r >= c).astype(BF16)


def _attn_kernel(bias_ref, q_ref, k_ref, v_ref, z_ref, m_ref, o_ref,
                 kb_scr, vb_scr, qm_scr, carry_scr, acc_scr, *, bq, bk):
    hp = pl.program_id(1)
    i = pl.program_id(2)
    lanes = 2 * SB_HEAD_DIM
    lane = lax.broadcasted_iota(jnp.int32, (bq, lanes), 1)

    @pl.when(i == 0)
    def _():
        kb_scr[...] = k_ref[...].astype(BF16)
        vb_scr[...] = v_ref[...].astype(BF16)

    q = q_ref[...] * SB_SCALE
    qm_scr[0] = jnp.where(lane < SB_HEAD_DIM, q, 0.0).astype(BF16)
    qm_scr[1] = jnp.where(lane >= SB_HEAD_DIM, q, 0.0).astype(BF16)
    carry_scr[...] = jnp.zeros_like(carry_scr)
    acc_scr[...] = jnp.zeros_like(acc_scr)

    def tile(j, masked):
        ks = pl.multiple_of(j * bk, bk)
        kb = kb_scr[pl.ds(ks, bk), :]
        vb = vb_scr[pl.ds(ks, bk), :]
        if masked:
            qpos = i * bq + lax.broadcasted_iota(jnp.int32, (bq, bk), 0)
            kpos = j * bk + lax.broadcasted_iota(jnp.int32, (bq, bk), 1)
            causal = kpos < qpos
        for h in range(2):
            z = lax.dot_general(qm_scr[h], kb, (((1,), (1,)), ((), ())), preferred_element_type=F32)
            z = z + bias_ref[2 * hp + h]
            sp = _softplus(z)
            lk = -sp
            if masked:
                lk = jnp.where(causal, lk, 0.0)
            sinc = jnp.dot(lk.astype(BF16), m_ref[...], preferred_element_type=F32)
            carry = carry_scr[h]
            between = (sinc - lk) + jnp.concatenate([carry] * (bk // lanes), axis=1)
            w = jnp.exp((z - sp) + between)
            if masked:
                w = jnp.where(causal, w, 0.0)
            acc_scr[h] += jnp.dot(w.astype(BF16), vb, preferred_element_type=F32)
            carry_scr[h] = carry + jnp.broadcast_to(sinc[:, 0:1], (bq, lanes))

    r = bq // bk
    for d in range(r):
        tile(r * i + (r - 1 - d), True)

    def body(t, c):
        tile(r * i - 1 - t, False)
        return c

    lax.fori_loop(0, r * i, body, 0)

    y = jnp.where(lane < SB_HEAD_DIM, acc_scr[0], acc_scr[1])
    o_ref[...] = y * _silu(z_ref[...])


def _attention_prompt(proj4, sb_bias, bq, bk):
    _, b, t, _ = proj4.shape
    lanes = 2 * SB_HEAD_DIM
    grid_spec = pltpu.PrefetchScalarGridSpec(
        num_scalar_prefetch=1,
        grid=(b, HEAD_PAIRS, t // bq),
        in_specs=[pl.BlockSpec((None, None, bq, lanes), lambda bb, hp, i, s: (4, bb, i, hp)),
                  pl.BlockSpec((None, None, t, lanes), lambda bb, hp, i, s: (5, bb, 0, hp)),
                  pl.BlockSpec((None, None, t, lanes), lambda bb, hp, i, s: (6, bb, 0, hp)),
                  pl.BlockSpec((None, None, bq, lanes), lambda bb, hp, i, s: (7, bb, i, hp)),
                  pl.BlockSpec((bk, bk), lambda bb, hp, i, s: (0, 0))],
        out_specs=pl.BlockSpec((None, bq, lanes), lambda bb, hp, i, s: (bb, i, hp)),
        scratch_shapes=[pltpu.VMEM((t, lanes), BF16), pltpu.VMEM((t, lanes), BF16),
                        pltpu.VMEM((2, bq, lanes), BF16),
                        pltpu.VMEM((2, bq, lanes), F32), pltpu.VMEM((2, bq, lanes), F32)])
    return pl.pallas_call(
        functools.partial(_attn_kernel, bq=bq, bk=bk),
        grid_spec=grid_spec,
        out_shape=jax.ShapeDtypeStruct((b, t, BRANCH_W), F32),
        compiler_params=_cparams("parallel", "parallel", "arbitrary"),
    )(sb_bias, proj4, proj4, proj4, proj4, _suffix_ones(bk))


def _decode_kernel(pt_ref, q_ref, z_ref, k_ref, v_ref, bias_ref, m_ref, o_ref, qm_scr, carry_scr, acc_scr):
    p = pl.program_id(1)
    page = k_ref.shape[0]
    head_of_lane = lax.broadcasted_iota(jnp.int32, (SB_HEADS, BRANCH_W), 1) // SB_HEAD_DIM
    own = head_of_lane == lax.broadcasted_iota(jnp.int32, (SB_HEADS, BRANCH_W), 0)

    @pl.when(p == 0)
    def _():
        q = jnp.broadcast_to(q_ref[...] * SB_SCALE, (SB_HEADS, BRANCH_W))
        qm_scr[...] = jnp.where(own, q, 0.0).astype(BF16)
        carry_scr[...] = jnp.zeros_like(carry_scr)
        acc_scr[...] = jnp.zeros_like(acc_scr)

    z = lax.dot_general(qm_scr[...], k_ref[...].astype(BF16), (((1,), (1,)), ((), ())),
                        preferred_element_type=F32)
    z = z + bias_ref[...]
    sp = _softplus(z)
    lk = -sp
    hi = lk.astype(BF16)
    lo = (lk - hi.astype(F32)).astype(BF16)
    sinc = (jnp.dot(hi, m_ref[...], preferred_element_type=F32)
            + jnp.dot(lo, m_ref[...], preferred_element_type=F32))
    carry = carry_scr[...]
    w = jnp.exp((z - sp) + ((sinc - lk) + carry))
    acc_scr[...] += jnp.dot(w.astype(BF16), v_ref[...].astype(BF16), preferred_element_type=F32)
    carry_scr[...] = carry + jnp.broadcast_to(sinc[:, 0:1], (SB_HEADS, page))

    @pl.when(p == pl.num_programs(1) - 1)
    def _():
        y = jnp.sum(jnp.where(own, acc_scr[...], 0.0), axis=0, keepdims=True)
        o_ref[...] = y * _silu(z_ref[...])


def _attention_sample(proj_s, cache_k, cache_v, page_table, bias_col, layer):
    s = proj_s.shape[1]
    n_pages = page_table.shape[1]
    depth, n_pool, page, _, _ = cache_k.shape
    ck = cache_k.reshape(depth, n_pool, page, BRANCH_W)
    cv = cache_v.reshape(depth, n_pool, page, BRANCH_W)
    rows = proj_s.reshape(N_PROJ, s, 1, BRANCH_W)
    page_of = lambda bb, p, pt: (layer, pt[bb, n_pages - 1 - p], 0, 0)
    grid_spec = pltpu.PrefetchScalarGridSpec(
        num_scalar_prefetch=1,
        grid=(s, n_pages),
        in_specs=[pl.BlockSpec((None, None, 1, BRANCH_W), lambda bb, p, pt: (4, bb, 0, 0)),
                  pl.BlockSpec((None, None, 1, BRANCH_W), lambda bb, p, pt: (7, bb, 0, 0)),
                  pl.BlockSpec((None, None, page, BRANCH_W), page_of),
                  pl.BlockSpec((None, None, page, BRANCH_W), page_of),
                  pl.BlockSpec((SB_HEADS, 1), lambda bb, p, pt: (0, 0)),
                  pl.BlockSpec((page, page), lambda bb, p, pt: (0, 0))],
        out_specs=pl.BlockSpec((None, 1, BRANCH_W), lambda bb, p, pt: (bb, 0, 0)),
        scratch_shapes=[pltpu.VMEM((SB_HEADS, BRANCH_W), BF16),
                        pltpu.VMEM((SB_HEADS, page), F32),
                        pltpu.VMEM((SB_HEADS, BRANCH_W), F32)])
    out = pl.pallas_call(
        _decode_kernel,
        grid_spec=grid_spec,
        out_shape=jax.ShapeDtypeStruct((s, 1, BRANCH_W), F32),
        compiler_params=_cparams("parallel", "arbitrary"),
    )(page_table, rows, rows, ck, cv, bias_col, _suffix_ones(page))
    return out.reshape(s, BRANCH_W)


def _merge_kernel(ya_ref, yb_ref, yc_ref, ga_ref, gb_ref, gc_ref, x_ref, wb_ref, wo_ref, g_ref, b_ref, o_ref,
                  *, alpha):
    merged = None
    for n, (y_ref, gate_ref) in enumerate(((ya_ref, ga_ref), (yb_ref, gb_ref), (yc_ref, gc_ref))):
        pb = jnp.dot(y_ref[...].astype(BF16), wb_ref[n], preferred_element_type=F32)
        term = jax.nn.sigmoid(gate_ref[...]) * pb
        merged = term if merged is None else merged + term
    out = jnp.dot(merged.astype(BF16), wo_ref[...], preferred_element_type=F32)
    v = alpha * x_ref[...] + out
    mu = jnp.mean(v, axis=-1, keepdims=True)
    c = v - mu
    var = jnp.mean(c * c, axis=-1, keepdims=True)
    o_ref[...] = c * lax.rsqrt(var + LN_EPS) * g_ref[...] + b_ref[...]


def _merge(ya, yb, yc, proj, x2d, w_branch_bf16, w_out_bf16, ln_g, ln_b, alpha, tm):
    n = x2d.shape[0]
    row = pl.BlockSpec((tm, D_MODEL), lambda i: (i, 0))
    part = lambda k: pl.BlockSpec((None, tm, D_MODEL), lambda i: (k, i, 0))
    vec = pl.BlockSpec((1, D_MODEL), lambda i: (0, 0))
    return pl.pallas_call(
        functools.partial(_merge_kernel, alpha=alpha),
        grid=(n // tm,),
        in_specs=[row, row, row, part(8), part(9), part(10), row,
                  pl.BlockSpec((3, BRANCH_W, D_MODEL), lambda i: (0, 0, 0)),
                  pl.BlockSpec((D_MODEL, D_MODEL), lambda i: (0, 0)),
                  vec, vec],
        out_specs=row,
        out_shape=jax.ShapeDtypeStruct((n, D_MODEL), F32),
        compiler_params=_cparams("parallel"),
    )(ya, yb, yc, proj, proj, proj, x2d, w_branch_bf16, w_out_bf16, ln_g, ln_b)


def _tile(n, pref):
    return pref if n % pref == 0 else n


def kernel(x_prompt, x_sample, cache_k, cache_v, state_pool, state_s5_re, state_s5_im, page_table, w_in, pool_w, pool_scale, s5_a_re, s5_a_im, s5_b_re, s5_b_im, s5_c_re, s5_c_im, s5_d, s5_log_dt, w_glu, b_glu, sb_bias, w_branch, w_out, ln_g, ln_b):
    depth = w_in.shape[0]
    alpha = (2.0 * depth) ** 0.25
    bsz, t_len, _ = x_prompt.shape
    dec_b = x_sample.shape[0]
    past_len = page_table.shape[1] * cache_k.shape[2]
    n_prompt = bsz * t_len

    xp = x_prompt.reshape(n_prompt, D_MODEL)
    xs = x_sample.reshape(dec_b, D_MODEL)
    outs = [[] for _ in range(10)]
    for l in range(depth):
        w_in_b = w_in[l].astype(BF16)
        pool_w_b = pool_w[l].astype(BF16)
        w_glu_b = w_glu[l].astype(BF16)
        w_branch_b = w_branch[l].astype(BF16)
        w_out_b = w_out[l].astype(BF16)
        row = lambda a: a[l].reshape(1, -1)
        s5_args = (s5_a_re[l], s5_a_im[l], s5_b_re[l], s5_b_im[l], s5_c_re[l], s5_c_im[l], s5_log_dt[l])

        proj = _project(xp, w_in_b, _tile(n_prompt, 1024))
        proj4 = proj.reshape(N_PROJ, bsz, t_len, BRANCH_W)
        ya = _pool_prompt(proj4, pool_w_b, row(pool_scale), _tile(t_len, 512)).reshape(n_prompt, BRANCH_W)
        y5, h_last = _s5_prompt(proj4[2], _s5_chunk_operators(*s5_args))
        yb = _glu(y5.reshape(n_prompt, BRANCH_W), proj, row(s5_d), w_glu_b, row(b_glu), _tile(n_prompt, 512))
        yc = _attention_prompt(proj4, sb_bias[l], _tile(t_len, 512), 256).reshape(n_prompt, BRANCH_W)
        xp = _merge(ya, yb, yc, proj, xp, w_branch_b, w_out_b, row(ln_g), row(ln_b), alpha, _tile(n_prompt, 256))
        outs[0].append(proj4[5].reshape(bsz, t_len, SB_HEADS, SB_HEAD_DIM))
        outs[1].append(proj4[6].reshape(bsz, t_len, SB_HEADS, SB_HEAD_DIM))
        outs[2].append(proj4[0][:, t_len - POOL_BUF:, :])
        half = 2 * S5_STATE
        outs[3].append(h_last[:, :, :half].reshape(bsz, S5_GROUPS, S5_STATE))
        outs[4].append(h_last[:, :, half:].reshape(bsz, S5_GROUPS, S5_STATE))

        proj_s = _project(xs, w_in_b, dec_b)
        prev_t = state_pool[l].transpose(1, 0, 2)
        ya_s = _pool_sample(proj_s, prev_t, pool_w_b, row(pool_scale), past_len)
        h_re = state_s5_re[l].reshape(dec_b, S5_GROUPS * S5_STATE)
        h_im = state_s5_im[l].reshape(dec_b, S5_GROUPS * S5_STATE)
        y5_s, n_re, n_im = _s5_sample(proj_s, h_re, h_im, _s5_sample_operators(*s5_args))
        yb_s = _glu(y5_s, proj_s, row(s5_d), w_glu_b, row(b_glu), dec_b)
        yc_s = _attention_sample(proj_s, cache_k, cache_v, page_table, sb_bias[l].reshape(SB_HEADS, 1), l)
        xs = _merge(ya_s, yb_s, yc_s, proj_s, xs, w_branch_b, w_out_b, row(ln_g), row(ln_b), alpha, dec_b)
        outs[5].append(proj_s[5].reshape(dec_b, 1, SB_HEADS, SB_HEAD_DIM))
        outs[6].append(proj_s[6].reshape(dec_b, 1, SB_HEADS, SB_HEAD_DIM))
        outs[7].append(jnp.concatenate([state_pool[l][:, 1:, :], proj_s[0][:, None, :]], axis=1))
        outs[8].append(n_re.reshape(dec_b, S5_GROUPS, S5_STATE))
        outs[9].append(n_im.reshape(dec_b, S5_GROUPS, S5_STATE))

    return (xp.reshape(bsz, t_len, D_MODEL), xs.reshape(dec_b, 1, D_MODEL)) + tuple(jnp.stack(o) for o in outs)
```

```python
import functools
import math

import jax
import jax.numpy as jnp
from jax import lax
from jax.experimental import pallas as pl
from jax.experimental.pallas import tpu as pltpu

F32 = jnp.float32
BF16 = jnp.bfloat16
HIGHEST = lax.Precision.HIGHEST

D_MODEL = 1024
BRANCH_W = 1024
N_PROJ = 11
MAIN_PARTS = (0, 1, 2, 3, 4, 7, 8, 9, 10)
N_MAIN = len(MAIN_PARTS)
P_UA, P_ZA, P_UB, P_ZB, P_Q, P_ZC, P_GA, P_GB, P_GC = range(N_MAIN)
KV_PARTS = (5, 6)
LOG2E = 1.4426950408889634
POOL_WINDOWS = (2, 4, 8, 16)
POOL_GW = BRANCH_W // len(POOL_WINDOWS)
POOL_BUF = max(POOL_WINDOWS) - 1
POOL_HALO = 16
S5_GROUP = 16
S5_GROUPS = BRANCH_W // S5_GROUP
S5_STATE = 64
S5_CHUNK = 16
S5_PAIRS = S5_GROUPS // 2
S5_PAIR_W = 2 * S5_GROUP
SB_HEADS = 16
SB_HEAD_DIM = BRANCH_W // SB_HEADS
SB_SCALE = 1.0 / math.sqrt(SB_HEAD_DIM)
HEAD_PAIRS = SB_HEADS // 2
PAIR_LANES = 2 * SB_HEAD_DIM
LN_EPS = 1e-5
VMEM_LIMIT = 56 * 1024 * 1024


def _cparams(*sem):
    return pltpu.CompilerParams(dimension_semantics=sem, vmem_limit_bytes=VMEM_LIMIT)


def _silu(z):
    return z * jax.nn.sigmoid(z)


def _softplus(z):
    return jnp.maximum(z, 0.0) + jnp.log(1.0 + jnp.exp2(jnp.abs(z) * (-LOG2E)))


def _proj_kernel(*refs):
    x_ref, wm_ref, wt_ref = refs[:3]
    o_ref, ot_ref, xb_ref = refs[-3:]
    j = pl.program_id(1)

    @pl.when(j == 0)
    def _():
        xb_ref[...] = x_ref[...].astype(BF16)

    @pl.when(j < N_MAIN)
    def _():
        o_ref[...] = jnp.dot(xb_ref[...], wm_ref[...], preferred_element_type=F32)

    @pl.when(j >= N_MAIN)
    def _():
        ot_ref[...] = lax.dot_general(wt_ref[...], xb_ref[...], (((1,), (1,)), ((), ())),
                                      preferred_element_type=F32)


def _project(x2d, w_main, w_kv_t, kv_buf, layer, depth, bsz, t_len, tm):
    n = x2d.shape[0]
    tpb = t_len // tm
    n_kv = len(KV_PARTS)
    main_j = lambda j: jnp.minimum(j, N_MAIN - 1)
    kv_j = lambda j: jnp.maximum(j - N_MAIN, 0)
    in_specs = [pl.BlockSpec((tm, D_MODEL), lambda i, j: (i, 0)),
                pl.BlockSpec((D_MODEL, BRANCH_W), lambda i, j: (0, main_j(j))),
                pl.BlockSpec((None, BRANCH_W, D_MODEL), lambda i, j: (kv_j(j), 0, 0))]
    args = [x2d, w_main, w_kv_t]
    aliases = {}
    if kv_buf is not None:
        in_specs.append(pl.BlockSpec(memory_space=pl.ANY))
        args.append(kv_buf)
        aliases = {3: 1}
    return pl.pallas_call(
        _proj_kernel,
        grid=(n // tm, N_MAIN + n_kv),
        in_specs=in_specs,
        out_specs=[pl.BlockSpec((None, tm, BRANCH_W), lambda i, j: (main_j(j), i, 0)),
                   pl.BlockSpec((None, None, None, BRANCH_W, tm),
                                lambda i, j: (layer, kv_j(j), i // tpb, 0, i % tpb))],
        out_shape=[jax.ShapeDtypeStruct((N_MAIN, n, BRANCH_W), F32),
                   jax.ShapeDtypeStruct((depth, n_kv, bsz, BRANCH_W, t_len), F32)],
        scratch_shapes=[pltpu.VMEM((tm, D_MODEL), BF16)],
        input_output_aliases=aliases,
        compiler_params=_cparams("parallel", "arbitrary"),
    )(*args)


def _pool_mix(acc, cnt, u, z, w_ref, sc_ref, g):
    sl = slice(g * POOL_GW, (g + 1) * POOL_GW)
    pooled = acc / cnt - u
    mixed = jnp.dot(pooled.astype(BF16), w_ref[g], preferred_element_type=F32)
    return mixed * sc_ref[:, sl] * _silu(z)


def _pool_prompt_kernel(u_ref, halo_ref, z_ref, w_ref, sc_ref, o_ref, ext_ref, *, tm):
    i = pl.program_id(1)
    ext_ref[0:POOL_HALO, :] = jnp.where(i == 0, 0.0, halo_ref[...])
    ext_ref[POOL_HALO:POOL_HALO + tm, :] = u_ref[...]
    pos = i * tm + lax.broadcasted_iota(jnp.int32, (tm, 1), 0)
    for g, w in enumerate(POOL_WINDOWS):
        sl = slice(g * POOL_GW, (g + 1) * POOL_GW)
        acc = ext_ref[POOL_HALO:POOL_HALO + tm, sl]
        for k in range(1, w):
            acc = acc + ext_ref[POOL_HALO - k:POOL_HALO - k + tm, sl]
        cnt = jnp.minimum(pos + 1, w).astype(F32)
        o_ref[:, sl] = _pool_mix(acc, cnt, u_ref[:, sl], z_ref[:, sl], w_ref, sc_ref, g)


def _pool_prompt(proj4, pool_w_bf16, pool_scale, tm):
    _, b, t, _ = proj4.shape
    hb = tm // POOL_HALO
    return pl.pallas_call(
        functools.partial(_pool_prompt_kernel, tm=tm),
        grid=(b, t // tm),
        in_specs=[pl.BlockSpec((None, None, tm, BRANCH_W), lambda bb, i: (P_UA, bb, i, 0)),
                  pl.BlockSpec((None, None, POOL_HALO, BRANCH_W),
                               lambda bb, i: (P_UA, bb, jnp.maximum(i * hb - 1, 0), 0)),
                  pl.BlockSpec((None, None, tm, BRANCH_W), lambda bb, i: (P_ZA, bb, i, 0)),
                  pl.BlockSpec((len(POOL_WINDOWS), POOL_GW, POOL_GW), lambda bb, i: (0, 0, 0)),
                  pl.BlockSpec((1, BRANCH_W), lambda bb, i: (0, 0))],
        out_specs=pl.BlockSpec((None, tm, BRANCH_W), lambda bb, i: (bb, i, 0)),
        out_shape=jax.ShapeDtypeStruct((b, t, BRANCH_W), F32),
        scratch_shapes=[pltpu.VMEM((POOL_HALO + tm, BRANCH_W), F32)],
        compiler_params=_cparams("parallel", "parallel"),
    )(proj4, proj4, proj4, pool_w_bf16, pool_scale)


def _pool_sample_kernel(u_ref, prev_ref, z_ref, w_ref, sc_ref, o_ref, *, pos0):
    for g, w in enumerate(POOL_WINDOWS):
        sl = slice(g * POOL_GW, (g + 1) * POOL_GW)
        acc = u_ref[:, sl]
        for k in range(1, w):
            acc = acc + prev_ref[POOL_BUF - k, :, sl]
        cnt = float(min(pos0 + 1, w))
        o_ref[:, sl] = _pool_mix(acc, cnt, u_ref[:, sl], z_ref[:, sl], w_ref, sc_ref, g)


def _pool_sample(proj_s, prev_t, pool_w_bf16, pool_scale, pos0):
    s = proj_s.shape[1]
    return pl.pallas_call(
        functools.partial(_pool_sample_kernel, pos0=pos0),
        grid=(1,),
        in_specs=[pl.BlockSpec((None, s, BRANCH_W), lambda i: (P_UA, 0, 0)),
                  pl.BlockSpec((POOL_BUF, s, BRANCH_W), lambda i: (0, 0, 0)),
                  pl.BlockSpec((None, s, BRANCH_W), lambda i: (P_ZA, 0, 0)),
                  pl.BlockSpec((len(POOL_WINDOWS), POOL_GW, POOL_GW), lambda i: (0, 0, 0)),
                  pl.BlockSpec((1, BRANCH_W), lambda i: (0, 0))],
        out_specs=pl.BlockSpec((s, BRANCH_W), lambda i: (0, 0)),
        out_shape=jax.ShapeDtypeStruct((s, BRANCH_W), F32),
        compiler_params=_cparams("arbitrary"),
    )(proj_s, prev_t, proj_s, pool_w_bf16, pool_scale)


def _s5_discretize(a_re, a_im, b_re, b_im, c_re, c_im, log_dt):
    lam = lax.complex(a_re, a_im)
    step = lam * jnp.exp(log_dt)[:, None]
    a_bar = jnp.exp(step)
    b_bar = ((a_bar - 1.0) / lam)[..., None] * lax.complex(b_re, b_im)
    c_mat = lax.complex(c_re, c_im)
    return step, a_bar, b_bar, c_mat


def _pair_blockdiag(m):
    g, big_r, r, big_c, c = m.shape
    out = jnp.zeros((g // 2, big_r, 2, r, big_c, 2, c), m.dtype)
    out = out.at[:, :, 0, :, :, 0, :].set(m[0::2])
    out = out.at[:, :, 1, :, :, 1, :].set(m[1::2])
    return out.reshape(g // 2, big_r * 2 * r, big_c * 2 * c)


def _s5_chunk_operators(a_re, a_im, b_re, b_im, c_re, c_im, log_dt):
    ell = S5_CHUNK
    step, _, b_bar, c_mat = _s5_discretize(a_re, a_im, b_re, b_im, c_re, c_im, log_dt)
    apow = jnp.exp(step[None] * jnp.arange(ell + 1, dtype=F32)[:, None, None])
    ca = c_mat[None] * apow[:, :, None, :]
    ca_re, ca_im = jnp.real(ca), jnp.imag(ca)
    bb_re, bb_im = jnp.real(b_bar), jnp.imag(b_bar)
    kmat = (jnp.einsum('jgop,gpi->gjoi', ca_re[:ell], bb_re, precision=HIGHEST)
            - jnp.einsum('jgop,gpi->gjoi', ca_im[:ell], bb_im, precision=HIGHEST))
    s_idx = jnp.arange(ell)[:, None]
    t_idx = jnp.arange(ell)[None, :]
    lag = jnp.clip(t_idx - s_idx, 0, ell - 1)
    tfull = jnp.where((t_idx >= s_idx)[None, :, :, None, None], kmat[:, lag], 0.0)
    tfull = tfull.transpose(0, 1, 4, 2, 3)
    t_op = _pair_blockdiag(tfull)
    w_c = apow[ell - 1 - jnp.arange(ell)][:, :, :, None] * b_bar[None]
    w_c = w_c.transpose(1, 0, 3, 2)
    w_re = _pair_blockdiag(jnp.real(w_c)[:, :, :, None, :])
    w_im = _pair_blockdiag(jnp.imag(w_c)[:, :, :, None, :])
    w_op = jnp.concatenate([w_re, w_im], axis=2)
    v_re = ca_re[1:].transpose(1, 3, 0, 2)
    v_im = -ca_im[1:].transpose(1, 3, 0, 2)
    v_re = _pair_blockdiag(v_re[:, None])
    v_im = _pair_blockdiag(v_im[:, None])
    v_op = jnp.concatenate([v_re, v_im], axis=1)
    a_l = apow[ell]
    a_op = jnp.concatenate([jnp.real(a_l).reshape(S5_PAIRS, 2 * S5_STATE),
                            jnp.imag(a_l).reshape(S5_PAIRS, 2 * S5_STATE)], axis=1)
    return t_op.astype(BF16), w_op.astype(BF16), v_op.astype(BF16), a_op


def _s5_state_kernel(u_ref, w_ref, o_ref):
    o_ref[...] = jnp.dot(u_ref[...], w_ref[...], preferred_element_type=F32)


def _s5_chunk_states(u2, w_op):
    gp, r, kw = u2.shape
    nw = w_op.shape[2]
    return pl.pallas_call(
        _s5_state_kernel,
        grid=(gp,),
        in_specs=[pl.BlockSpec((None, r, kw), lambda g: (g, 0, 0)),
                  pl.BlockSpec((None, kw, nw), lambda g: (g, 0, 0))],
        out_specs=pl.BlockSpec((None, r, nw), lambda g: (g, 0, 0)),
        out_shape=jax.ShapeDtypeStruct((gp, r, nw), F32),
        compiler_params=_cparams("parallel"),
    )(u2, w_op)


def _s5_scan_kernel(s_ref, a_ref, hprev_ref, hlast_ref, h_scr, *, cb):
    j = pl.program_id(1)
    half = 2 * S5_STATE

    @pl.when(j == 0)
    def _():
        h_scr[...] = jnp.zeros_like(h_scr)

    a_r = a_ref[:, :half]
    a_i = a_ref[:, half:]

    def body(c, carry):
        h_r, h_i = carry
        hprev_ref[c, :, :half] = h_r.astype(BF16)
        hprev_ref[c, :, half:] = h_i.astype(BF16)
        s = s_ref[c]
        n_r = a_r * h_r - a_i * h_i + s[:, :half]
        n_i = a_r * h_i + a_i * h_r + s[:, half:]
        return n_r, n_i

    h_r, h_i = lax.fori_loop(0, cb, body, (h_scr[:, :half], h_scr[:, half:]))
    h_scr[:, :half] = h_r
    h_scr[:, half:] = h_i

    @pl.when(j == pl.num_programs(1) - 1)
    def _():
        hlast_ref[...] = h_scr[...]


def _s5_scan(s_t, a_op, cb):
    b, nc, gp, w = s_t.shape
    return pl.pallas_call(
        functools.partial(_s5_scan_kernel, cb=cb),
        grid=(b, nc // cb),
        in_specs=[pl.BlockSpec((None, cb, gp, w), lambda bb, j: (bb, j, 0, 0)),
                  pl.BlockSpec((gp, w), lambda bb, j: (0, 0))],
        out_specs=[pl.BlockSpec((None, cb, gp, w), lambda bb, j: (bb, j, 0, 0)),
                   pl.BlockSpec((None, gp, w), lambda bb, j: (bb, 0, 0))],
        out_shape=[jax.ShapeDtypeStruct((b, nc, gp, w), BF16),
                   jax.ShapeDtypeStruct((b, gp, w), F32)],
        scratch_shapes=[pltpu.VMEM((gp, w), F32)],
        compiler_params=_cparams("parallel", "arbitrary"),
    )(s_t, a_op)


def _s5_out_kernel(u_ref, h_ref, t_ref, v_ref, o_ref):
    o_ref[...] = (jnp.dot(u_ref[...], t_ref[...], preferred_element_type=F32)
                  + jnp.dot(h_ref[...], v_ref[...], preferred_element_type=F32))


def _s5_chunk_outputs(u2, h2, t_op, v_op):
    gp, r, kw = u2.shape
    hw = h2.shape[2]
    return pl.pallas_call(
        _s5_out_kernel,
        grid=(gp,),
        in_specs=[pl.BlockSpec((None, r, kw), lambda g: (g, 0, 0)),
                  pl.BlockSpec((None, r, hw), lambda g: (g, 0, 0)),
                  pl.BlockSpec((None, kw, kw), lambda g: (g, 0, 0)),
                  pl.BlockSpec((None, hw, kw), lambda g: (g, 0, 0))],
        out_specs=pl.BlockSpec((None, r, kw), lambda g: (g, 0, 0)),
        out_shape=jax.ShapeDtypeStruct((gp, r, kw), F32),
        compiler_params=_cparams("parallel"),
    )(u2, h2, t_op, v_op)


def _s5_prompt(u_b, ops):
    t_op, w_op, v_op, a_op = ops
    b, t, _ = u_b.shape
    nc = t // S5_CHUNK
    u2 = u_b.reshape(b, nc, S5_CHUNK, S5_PAIRS, S5_PAIR_W).transpose(3, 0, 1, 2, 4)
    u2 = u2.reshape(S5_PAIRS, b * nc, S5_CHUNK * S5_PAIR_W).astype(BF16)
    s_loc = _s5_chunk_states(u2, w_op)
    s_t = s_loc.reshape(S5_PAIRS, b, nc, -1).transpose(1, 2, 0, 3)
    h_prev, h_last = _s5_scan(s_t, a_op, min(nc, 128))
    h2 = h_prev.transpose(2, 0, 1, 3).reshape(S5_PAIRS, b * nc, -1)
    y2 = _s5_chunk_outputs(u2, h2, t_op, v_op)
    y = y2.reshape(S5_PAIRS, b, nc, S5_CHUNK, S5_PAIR_W).transpose(1, 2, 3, 0, 4)
    return y.reshape(b, t, BRANCH_W), h_last


def _s5_sample_operators(a_re, a_im, b_re, b_im, c_re, c_im, log_dt):
    _, a_bar, b_bar, c_mat = _s5_discretize(a_re, a_im, b_re, b_im, c_re, c_im, log_dt)
    ns = BRANCH_W // 256
    gs = S5_GROUPS // ns
    eye = jnp.eye(gs, dtype=F32)

    def expand_b(m):
        m = m.reshape(ns, gs, S5_STATE, S5_GROUP).transpose(0, 1, 3, 2)
        m = m[:, :, :, None, :] * eye[None, :, None, :, None]
        return m.reshape(ns, gs * S5_GROUP, gs * S5_STATE)

    def expand_c(m):
        m = m.reshape(ns, gs, S5_GROUP, S5_STATE).transpose(0, 1, 3, 2)
        m = m[:, :, :, None, :] * eye[None, :, None, :, None]
        return m.reshape(ns, gs * S5_STATE, gs * S5_GROUP)

    b_op = jnp.concatenate([expand_b(jnp.real(b_bar)), expand_b(jnp.imag(b_bar))], axis=2)
    c_op = jnp.concatenate([expand_c(jnp.real(c_mat)), expand_c(-jnp.imag(c_mat))], axis=1)
    a_r = jnp.real(a_bar).reshape(1, S5_GROUPS * S5_STATE)
    a_i = jnp.imag(a_bar).reshape(1, S5_GROUPS * S5_STATE)
    return b_op.astype(BF16), c_op.astype(BF16), a_r, a_i


def _s5_sample_kernel(u_ref, hr_ref, hi_ref, ar_ref, ai_ref, b_ref, c_ref, y_ref, or_ref, oi_ref):
    ns = b_ref.shape[0]
    cw = b_ref.shape[1]
    sw = c_ref.shape[1] // 2
    for n in range(ns):
        cs = slice(n * cw, (n + 1) * cw)
        ss = slice(n * sw, (n + 1) * sw)
        bu = jnp.dot(u_ref[:, cs].astype(BF16), b_ref[n], preferred_element_type=F32)
        a_r, a_i = ar_ref[:, ss], ai_ref[:, ss]
        h_r, h_i = hr_ref[:, ss], hi_ref[:, ss]
        n_r = a_r * h_r - a_i * h_i + bu[:, :sw]
        n_i = a_r * h_i + a_i * h_r + bu[:, sw:]
        or_ref[:, ss] = n_r
        oi_ref[:, ss] = n_i
        hcat = jnp.concatenate([n_r, n_i], axis=1).astype(BF16)
        y_ref[:, cs] = jnp.dot(hcat, c_ref[n], preferred_element_type=F32)


def _s5_sample(proj_s, h_re, h_im, ops):
    b_op, c_op, a_r, a_i = ops
    s = proj_s.shape[1]
    sw = h_re.shape[1]
    full = lambda shape: pl.BlockSpec(shape, lambda i: (0,) * len(shape))
    return pl.pallas_call(
        _s5_sample_kernel,
        grid=(1,),
        in_specs=[pl.BlockSpec((None, s, BRANCH_W), lambda i: (P_UB, 0, 0)),
                  full((s, sw)), full((s, sw)), full((1, sw)), full((1, sw)),
                  full(b_op.shape), full(c_op.shape)],
        out_specs=[full((s, BRANCH_W)), full((s, sw)), full((s, sw))],
        out_shape=[jax.ShapeDtypeStruct((s, BRANCH_W), F32),
                   jax.ShapeDtypeStruct((s, sw), F32),
                   jax.ShapeDtypeStruct((s, sw), F32)],
        compiler_params=_cparams("arbitrary"),
    )(proj_s, h_re, h_im, a_r, a_i, b_op, c_op)


def _glu_kernel(y_ref, u_ref, z_ref, d_ref, w_ref, b_ref, o_ref):
    y = y_ref[...] + d_ref[...] * u_ref[...]
    g = jax.nn.gelu(y, approximate=True)
    t = jnp.dot(g.astype(BF16), w_ref[...], preferred_element_type=F32) + b_ref[...]
    o_ref[...] = g * jax.nn.sigmoid(t) * _silu(z_ref[...])


def _glu(y2d, proj, s5_d, w_glu_bf16, b_glu, tm):
    n = y2d.shape[0]
    return pl.pallas_call(
        _glu_kernel,
        grid=(n // tm,),
        in_specs=[pl.BlockSpec((tm, BRANCH_W), lambda i: (i, 0)),
                  pl.BlockSpec((None, tm, BRANCH_W), lambda i: (P_UB, i, 0)),
                  pl.BlockSpec((None, tm, BRANCH_W), lambda i: (P_ZB, i, 0)),
                  pl.BlockSpec((1, BRANCH_W), lambda i: (0, 0)),
                  pl.BlockSpec((BRANCH_W, BRANCH_W), lambda i: (0, 0)),
                  pl.BlockSpec((1, BRANCH_W), lambda i: (0, 0))],
        out_specs=pl.BlockSpec((tm, BRANCH_W), lambda i: (i, 0)),
        out_shape=jax.ShapeDtypeStruct((n, BRANCH_W), F32),
        compiler_params=_cparams("parallel"),
    )(y2d, proj, proj, s5_d, w_glu_bf16, b_glu)


def _neg_suffix_ones(n):
    r = lax.broadcasted_iota(jnp.int32, (n, n), 0)
    c = lax.broadcasted_iota(jnp.int32, (n, n), 1)
    return jnp.where(r >= c, -1.0, 0.0).astype(BF16)


def _attn_kernel(bias_ref, q_ref, kt_ref, vt_ref, z_ref, m_ref, o_ref,
                 kb_scr, vb_scr, qm_scr, carry_scr, acc_scr, *, bq, bk):
    hp = pl.program_id(1)
    i = pl.program_id(2)
    lane = lax.broadcasted_iota(jnp.int32, (bq, PAIR_LANES), 1)

    @pl.when(i == 0)
    def _():
        kb_scr[...] = kt_ref[...].astype(BF16)
        vb_scr[...] = vt_ref[...].astype(BF16)

    q = q_ref[...] * SB_SCALE
    qm_scr[0] = jnp.where(lane < SB_HEAD_DIM, q, 0.0).astype(BF16)
    qm_scr[1] = jnp.where(lane >= SB_HEAD_DIM, q, 0.0).astype(BF16)
    carry_scr[...] = jnp.zeros_like(carry_scr)
    acc_scr[...] = jnp.zeros_like(acc_scr)

    def tile(j, masked):
        ks = pl.multiple_of(j * bk, bk)
        kb = kb_scr[:, pl.ds(ks, bk)]
        vb = vb_scr[:, pl.ds(ks, bk)]
        if masked:
            qpos = i * bq + lax.broadcasted_iota(jnp.int32, (bq, bk), 0)
            kpos = j * bk + lax.broadcasted_iota(jnp.int32, (bq, bk), 1)
            causal = kpos < qpos
        for h in range(2):
            z = jnp.dot(qm_scr[h], kb, preferred_element_type=F32)
            z = z + bias_ref[2 * hp + h]
            sp = _softplus(z)
            if masked:
                sp = jnp.where(causal, sp, 0.0)
            nsuf = jnp.dot(sp.astype(BF16), m_ref[...], preferred_element_type=F32)
            carry = carry_scr[h]
            expo = (z + nsuf) + jnp.concatenate([carry] * (bk // PAIR_LANES), axis=1)
            w = jnp.exp2(expo * LOG2E)
            if masked:
                w = jnp.where(causal, w, 0.0)
            acc_scr[h] += lax.dot_general(w.astype(BF16), vb, (((1,), (1,)), ((), ())),
                                          preferred_element_type=F32)
            carry_scr[h] = carry + jnp.broadcast_to(nsuf[:, 0:1], (bq, PAIR_LANES))

    r = bq // bk
    for d in range(r):
        tile(r * i + (r - 1 - d), True)

    def body(t, c):
        tile(r * i - 1 - t, False)
        return c

    lax.fori_loop(0, r * i, body, 0)

    y = jnp.where(lane < SB_HEAD_DIM, acc_scr[0], acc_scr[1])
    o_ref[...] = y * _silu(z_ref[...])


def _attention_prompt(proj4, kv_buf, sb_bias, layer, bq, bk):
    _, b, t, _ = proj4.shape
    kv_spec = lambda which: pl.BlockSpec((None, None, None, PAIR_LANES, t),
                                         lambda bb, hp, i, s: (layer, which, bb, hp, 0))
    grid_spec = pltpu.PrefetchScalarGridSpec(
        num_scalar_prefetch=1,
        grid=(b, HEAD_PAIRS, t // bq),
        in_specs=[pl.BlockSpec((None, None, bq, PAIR_LANES), lambda bb, hp, i, s: (P_Q, bb, i, hp)),
                  kv_spec(0), kv_spec(1),
                  pl.BlockSpec((None, None, bq, PAIR_LANES), lambda bb, hp, i, s: (P_ZC, bb, i, hp)),
                  pl.BlockSpec((bk, bk), lambda bb, hp, i, s: (0, 0))],
        out_specs=pl.BlockSpec((None, bq, PAIR_LANES), lambda bb, hp, i, s: (bb, i, hp)),
        scratch_shapes=[pltpu.VMEM((PAIR_LANES, t), BF16), pltpu.VMEM((PAIR_LANES, t), BF16),
                        pltpu.VMEM((2, bq, PAIR_LANES), BF16),
                        pltpu.VMEM((2, bq, PAIR_LANES), F32), pltpu.VMEM((2, bq, PAIR_LANES), F32)])
    return pl.pallas_call(
        functools.partial(_attn_kernel, bq=bq, bk=bk),
        grid_spec=grid_spec,
        out_shape=jax.ShapeDtypeStruct((b, t, BRANCH_W), F32),
        compiler_params=_cparams("parallel", "parallel", "arbitrary"),
    )(sb_bias, proj4, kv_buf, kv_buf, proj4, _neg_suffix_ones(bk))


def _decode_kernel(pt_ref, qt_ref, z_ref, *refs, pps):
    k_refs = refs[:pps]
    v_refs = refs[pps:2 * pps]
    bias_ref, m_ref, o_ref, qb_scr, carry_scr, acc_scr = refs[2 * pps:]
    s = pl.program_id(1)
    page = k_refs[0].shape[2]

    @pl.when(s == 0)
    def _():
        qt = qt_ref[...] * SB_SCALE
        for h in range(SB_HEADS):
            qb_scr[h] = jnp.broadcast_to(qt[:, h:h + 1], (SB_HEAD_DIM, page))
        carry_scr[...] = jnp.zeros_like(carry_scr)
        acc_scr[...] = jnp.zeros_like(acc_scr)

    for p in range(pps):
        zt = jnp.concatenate([jnp.sum(k_refs[p][h] * qb_scr[h], axis=0, keepdims=True)
                              for h in range(SB_HEADS)], axis=0)
        z = zt + bias_ref[...]
        sp = _softplus(z)
        hi = sp.astype(BF16)
        lo = (sp - hi.astype(F32)).astype(BF16)
        nsuf = (jnp.dot(hi, m_ref[...], preferred_element_type=F32)
                + jnp.dot(lo, m_ref[...], preferred_element_type=F32))
        carry = carry_scr[...]
        w = jnp.exp2(((z + nsuf) + carry) * LOG2E)
        for h in range(SB_HEADS):
            acc_scr[h] += v_refs[p][h] * jnp.broadcast_to(w[h:h + 1, :], (SB_HEAD_DIM, page))
        carry_scr[...] = carry + jnp.broadcast_to(nsuf[:, 0:1], (SB_HEADS, page))

    @pl.when(s == pl.num_programs(1) - 1)
    def _():
        y = jnp.sum(acc_scr[...], axis=-1)
        o_ref[...] = y * _silu(z_ref[...])


def _attention_sample(proj_s, cache_k, cache_v, page_table, bias_col, layer):
    s = proj_s.shape[1]
    n_pages = page_table.shape[1]
    page = cache_k.shape[2]
    pps = next(c for c in (4, 2, 1) if n_pages % c == 0)
    ck = cache_k.transpose(0, 1, 3, 4, 2)
    cv = cache_v.transpose(0, 1, 3, 4, 2)
    heads = lambda part: proj_s[part].reshape(s, SB_HEADS, SB_HEAD_DIM)
    q_t = heads(P_Q).transpose(0, 2, 1)

    def page_spec(j):
        return pl.BlockSpec((None, None, SB_HEADS, SB_HEAD_DIM, page),
                            lambda bb, st, pt: (layer, pt[bb, n_pages - 1 - (st * pps + j)], 0, 0, 0))

    grid_spec = pltpu.PrefetchScalarGridSpec(
        num_scalar_prefetch=1,
        grid=(s, n_pages // pps),
        in_specs=[pl.BlockSpec((None, SB_HEAD_DIM, SB_HEADS), lambda bb, st, pt: (bb, 0, 0)),
                  pl.BlockSpec((None, SB_HEADS, SB_HEAD_DIM), lambda bb, st, pt: (bb, 0, 0))]
                 + [page_spec(j) for j in range(pps)] * 2
                 + [pl.BlockSpec((SB_HEADS, 1), lambda bb, st, pt: (0, 0)),
                    pl.BlockSpec((page, page), lambda bb, st, pt: (0, 0))],
        out_specs=pl.BlockSpec((None, SB_HEADS, SB_HEAD_DIM), lambda bb, st, pt: (bb, 0, 0)),
        scratch_shapes=[pltpu.VMEM((SB_HEADS, SB_HEAD_DIM, page), F32),
                        pltpu.VMEM((SB_HEADS, page), F32),
                        pltpu.VMEM((SB_HEADS, SB_HEAD_DIM, page), F32)])
    out = pl.pallas_call(
        functools.partial(_decode_kernel, pps=pps),
        grid_spec=grid_spec,
        out_shape=jax.ShapeDtypeStruct((s, SB_HEADS, SB_HEAD_DIM), F32),
        compiler_params=_cparams("parallel", "arbitrary"),
    )(page_table, q_t, heads(P_ZC), *([ck] * pps), *([cv] * pps), bias_col, _neg_suffix_ones(page))
    return out.reshape(s, BRANCH_W)


def _merge_kernel(ya_ref, yb_ref, yc_ref, ga_ref, gb_ref, gc_ref, x_ref, wb_ref, wo_ref, g_ref, b_ref, o_ref,
                  *, alpha):
    merged = None
    for n, (y_ref, gate_ref) in enumerate(((ya_ref, ga_ref), (yb_ref, gb_ref), (yc_ref, gc_ref))):
        pb = jnp.dot(y_ref[...].astype(BF16), wb_ref[n], preferred_element_type=F32)
        term = jax.nn.sigmoid(gate_ref[...]) * pb
        merged = term if merged is None else merged + term
    out = jnp.dot(merged.astype(BF16), wo_ref[...], preferred_element_type=F32)
    v = alpha * x_ref[...] + out
    mu = jnp.mean(v, axis=-1, keepdims=True)
    c = v - mu
    var = jnp.mean(c * c, axis=-1, keepdims=True)
    o_ref[...] = c * lax.rsqrt(var + LN_EPS) * g_ref[...] + b_ref[...]


def _merge(ya, yb, yc, proj, x2d, w_branch_bf16, w_out_bf16, ln_g, ln_b, alpha, tm):
    n = x2d.shape[0]
    row = pl.BlockSpec((tm, D_MODEL), lambda i: (i, 0))
    part = lambda k: pl.BlockSpec((None, tm, D_MODEL), lambda i: (k, i, 0))
    vec = pl.BlockSpec((1, D_MODEL), lambda i: (0, 0))
    return pl.pallas_call(
        functools.partial(_merge_kernel, alpha=alpha),
        grid=(n // tm,),
        in_specs=[row, row, row, part(P_GA), part(P_GB), part(P_GC), row,
                  pl.BlockSpec((3, BRANCH_W, D_MODEL), lambda i: (0, 0, 0)),
                  pl.BlockSpec((D_MODEL, D_MODEL), lambda i: (0, 0)),
                  vec, vec],
        out_specs=row,
        out_shape=jax.ShapeDtypeStruct((n, D_MODEL), F32),
        compiler_params=_cparams("parallel"),
    )(ya, yb, yc, proj, proj, proj, x2d, w_branch_bf16, w_out_bf16, ln_g, ln_b)


def _tile(n, pref):
    return pref if n % pref == 0 else n


def _split_w_in(w):
    parts = w.reshape(D_MODEL, N_PROJ, BRANCH_W)
    w_main = parts[:, MAIN_PARTS, :].reshape(D_MODEL, N_MAIN * BRANCH_W).astype(BF16)
    w_kv_t = parts[:, KV_PARTS, :].transpose(1, 2, 0).astype(BF16)
    return w_main, w_kv_t


def _heads_last(kv_t):
    d, b, _, t = kv_t.shape
    return kv_t.reshape(d, b, SB_HEADS, SB_HEAD_DIM, t).transpose(0, 1, 4, 2, 3)


def kernel(x_prompt, x_sample, cache_k, cache_v, state_pool, state_s5_re, state_s5_im, page_table, w_in, pool_w, pool_scale, s5_a_re, s5_a_im, s5_b_re, s5_b_im, s5_c_re, s5_c_im, s5_d, s5_log_dt, w_glu, b_glu, sb_bias, w_branch, w_out, ln_g, ln_b):
    depth = w_in.shape[0]
    alpha = (2.0 * depth) ** 0.25
    bsz, t_len, _ = x_prompt.shape
    dec_b = x_sample.shape[0]
    past_len = page_table.shape[1] * cache_k.shape[2]
    n_prompt = bsz * t_len

    xp = x_prompt.reshape(n_prompt, D_MODEL)
    xs = x_sample.reshape(dec_b, D_MODEL)
    kv_p = kv_s = None
    outs = [[] for _ in range(6)]
    for l in range(depth):
        w_main, w_kv_t = _split_w_in(w_in[l])
        pool_w_b = pool_w[l].astype(BF16)
        w_glu_b = w_glu[l].astype(BF16)
        w_branch_b = w_branch[l].astype(BF16)
        w_out_b = w_out[l].astype(BF16)
        row = lambda a: a[l].reshape(1, -1)
        s5_args = (s5_a_re[l], s5_a_im[l], s5_b_re[l], s5_b_im[l], s5_c_re[l], s5_c_im[l], s5_log_dt[l])

        proj, kv_p = _project(xp, w_main, w_kv_t, kv_p, l, depth, bsz, t_len, _tile(t_len, 1024))
        proj4 = proj.reshape(N_MAIN, bsz, t_len, BRANCH_W)
        ya = _pool_prompt(proj4, pool_w_b, row(pool_scale), _tile(t_len, 512)).reshape(n_prompt, BRANCH_W)
        y5, h_last = _s5_prompt(proj4[P_UB], _s5_chunk_operators(*s5_args))
        yb = _glu(y5.reshape(n_prompt, BRANCH_W), proj, row(s5_d), w_glu_b, row(b_glu), _tile(n_prompt, 512))
        yc = _attention_prompt(proj4, kv_p, sb_bias[l], l, _tile(t_len, 512), 256).reshape(n_prompt, BRANCH_W)
        xp = _merge(ya, yb, yc, proj, xp, w_branch_b, w_out_b, row(ln_g), row(ln_b), alpha, _tile(n_prompt, 256))
        outs[0].append(proj4[P_UA][:, t_len - POOL_BUF:, :])
        half = 2 * S5_STATE
        outs[1].append(h_last[:, :, :half].reshape(bsz, S5_GROUPS, S5_STATE))
        outs[2].append(h_last[:, :, half:].reshape(bsz, S5_GROUPS, S5_STATE))

        proj_s, kv_s = _project(xs, w_main, w_kv_t, kv_s, l, depth, 1, dec_b, dec_b)
        prev_t = state_pool[l].transpose(1, 0, 2)
        ya_s = _pool_sample(proj_s, prev_t, pool_w_b, row(pool_scale), past_len)
        h_re = state_s5_re[l].reshape(dec_b, S5_GROUPS * S5_STATE)
        h_im = state_s5_im[l].reshape(dec_b, S5_GROUPS * S5_STATE)
        y5_s, n_re, n_im = _s5_sample(proj_s, h_re, h_im, _s5_sample_operators(*s5_args))
        yb_s = _glu(y5_s, proj_s, row(s5_d), w_glu_b, row(b_glu), dec_b)
        yc_s = _attention_sample(proj_s, cache_k, cache_v, page_table, sb_bias[l].reshape(SB_HEADS, 1), l)
        xs = _merge(ya_s, yb_s, yc_s, proj_s, xs, w_branch_b, w_out_b, row(ln_g), row(ln_b), alpha, dec_b)
        outs[3].append(jnp.concatenate([prev_t[1:], proj_s[P_UA][None]], axis=0).transpose(1, 0, 2))
        outs[4].append(n_re.reshape(dec_b, S5_GROUPS, S5_STATE))
        outs[5].append(n_im.reshape(dec_b, S5_GROUPS, S5_STATE))

    pool_p, s5r_p, s5i_p, pool_s, s5r_s, s5i_s = (jnp.stack(o) for o in outs)
    k_p, v_p = _heads_last(kv_p[:, 0]), _heads_last(kv_p[:, 1])
    k_s = _heads_last(kv_s[:, 0]).transpose(0, 2, 1, 3, 4)
    v_s = _heads_last(kv_s[:, 1]).transpose(0, 2, 1, 3, 4)
    return (xp.reshape(bsz, t_len, D_MODEL), xs.reshape(dec_b, 1, D_MODEL),
            k_p, v_p, pool_p, s5r_p, s5i_p, k_s, v_s, pool_s, s5r_s, s5i_s)
```

```python
import functools
import math

import jax
import jax.numpy as jnp
from jax import lax
from jax.experimental import pallas as pl
from jax.experimental.pallas import tpu as pltpu

F32 = jnp.float32
BF16 = jnp.bfloat16
HIGHEST = lax.Precision.HIGHEST

D_MODEL = 1024
BRANCH_W = 1024
N_PROJ = 11
MAIN_PARTS = (0, 1, 2, 3, 4, 7, 8, 9, 10)
N_MAIN = len(MAIN_PARTS)
P_UA, P_ZA, P_UB, P_ZB, P_Q, P_ZC, P_GA, P_GB, P_GC = range(N_MAIN)
KV_PARTS = (5, 6)
LOG2E = 1.4426950408889634
POOL_WINDOWS = (2, 4, 8, 16)
POOL_GW = BRANCH_W // len(POOL_WINDOWS)
POOL_BUF = max(POOL_WINDOWS) - 1
POOL_HALO = 16
S5_GROUP = 16
S5_GROUPS = BRANCH_W // S5_GROUP
S5_STATE = 64
S5_CHUNK = 8
S5_SLAB_W = 128
S5_SLAB_G = S5_SLAB_W // S5_GROUP
S5_SLABS = BRANCH_W // S5_SLAB_W
S5_CHUNK_W = S5_CHUNK * S5_SLAB_W
SB_HEADS = 16
SB_HEAD_DIM = BRANCH_W // SB_HEADS
SB_SCALE = 1.0 / math.sqrt(SB_HEAD_DIM)
HEAD_PAIRS = SB_HEADS // 2
PAIR_LANES = 2 * SB_HEAD_DIM
LN_EPS = 1e-5
VMEM_LIMIT = 56 * 1024 * 1024


def _cparams(*sem):
    return pltpu.CompilerParams(dimension_semantics=sem, vmem_limit_bytes=VMEM_LIMIT)


def _silu(z):
    return z * jax.nn.sigmoid(z)


def _softplus(z):
    return jnp.maximum(z, 0.0) + jnp.log(1.0 + jnp.exp2(jnp.abs(z) * (-LOG2E)))


def _proj_kernel(*refs):
    x_ref, wm_ref, wt_ref = refs[:3]
    o_ref, kt_ref, vt_ref, xb_ref = refs[-4:]
    j = pl.program_id(1)

    @pl.when(j == 0)
    def _():
        xb_ref[...] = x_ref[...].astype(BF16)

    @pl.when(j < N_MAIN)
    def _():
        o_ref[...] = jnp.dot(xb_ref[...], wm_ref[...], preferred_element_type=F32)

    for part, t_ref in enumerate((kt_ref, vt_ref)):
        @pl.when(j == N_MAIN + part)
        def _():
            t_ref[...] = lax.dot_general(wt_ref[...], xb_ref[...], (((1,), (1,)), ((), ())),
                                         preferred_element_type=F32)


def _project(x2d, w_main, w_kv_t, kv_bufs, layer, depth, bsz, t_len, tm):
    n = x2d.shape[0]
    tpb = t_len // tm
    n_kv = len(KV_PARTS)
    main_j = lambda j: jnp.minimum(j, N_MAIN - 1)
    kv_j = lambda j: jnp.maximum(j - N_MAIN, 0)
    in_specs = [pl.BlockSpec((tm, D_MODEL), lambda i, j: (i, 0)),
                pl.BlockSpec((D_MODEL, BRANCH_W), lambda i, j: (0, main_j(j))),
                pl.BlockSpec((None, BRANCH_W, D_MODEL), lambda i, j: (kv_j(j), 0, 0))]
    args = [x2d, w_main, w_kv_t]
    aliases = {}
    if kv_bufs is not None:
        in_specs += [pl.BlockSpec(memory_space=pl.ANY)] * n_kv
        args += list(kv_bufs)
        aliases = {3: 1, 4: 2}
    kv_spec = pl.BlockSpec((None, None, BRANCH_W, tm), lambda i, j: (layer, i // tpb, 0, i % tpb))
    kv_shape = jax.ShapeDtypeStruct((depth, bsz, BRANCH_W, t_len), F32)
    main, k_buf, v_buf = pl.pallas_call(
        _proj_kernel,
        grid=(n // tm, N_MAIN + n_kv),
        in_specs=in_specs,
        out_specs=[pl.BlockSpec((None, tm, BRANCH_W), lambda i, j: (main_j(j), i, 0)), kv_spec, kv_spec],
        out_shape=[jax.ShapeDtypeStruct((N_MAIN, n, BRANCH_W), F32), kv_shape, kv_shape],
        scratch_shapes=[pltpu.VMEM((tm, D_MODEL), BF16)],
        input_output_aliases=aliases,
        compiler_params=_cparams("parallel", "arbitrary"),
    )(*args)
    return main, (k_buf, v_buf)


def _pool_mix(acc, cnt, u, z, w_ref, sc_ref, g):
    sl = slice(g * POOL_GW, (g + 1) * POOL_GW)
    pooled = acc / cnt - u
    mixed = jnp.dot(pooled.astype(BF16), w_ref[g], preferred_element_type=F32)
    return mixed * sc_ref[:, sl] * _silu(z)


def _pool_prompt_kernel(u_ref, halo_ref, z_ref, w_ref, sc_ref, o_ref, ext_ref, *, tm):
    i = pl.program_id(1)
    ext_ref[0:POOL_HALO, :] = jnp.where(i == 0, 0.0, halo_ref[...])
    ext_ref[POOL_HALO:POOL_HALO + tm, :] = u_ref[...]
    pos = i * tm + lax.broadcasted_iota(jnp.int32, (tm, 1), 0)
    for g, w in enumerate(POOL_WINDOWS):
        sl = slice(g * POOL_GW, (g + 1) * POOL_GW)
        acc = ext_ref[POOL_HALO:POOL_HALO + tm, sl]
        for k in range(1, w):
            acc = acc + ext_ref[POOL_HALO - k:POOL_HALO - k + tm, sl]
        cnt = jnp.minimum(pos + 1, w).astype(F32)
        o_ref[:, sl] = _pool_mix(acc, cnt, u_ref[:, sl], z_ref[:, sl], w_ref, sc_ref, g)


def _pool_prompt(proj4, pool_w_bf16, pool_scale, tm):
    _, b, t, _ = proj4.shape
    hb = tm // POOL_HALO
    return pl.pallas_call(
        functools.partial(_pool_prompt_kernel, tm=tm),
        grid=(b, t // tm),
        in_specs=[pl.BlockSpec((None, None, tm, BRANCH_W), lambda bb, i: (P_UA, bb, i, 0)),
                  pl.BlockSpec((None, None, POOL_HALO, BRANCH_W),
                               lambda bb, i: (P_UA, bb, jnp.maximum(i * hb - 1, 0), 0)),
                  pl.BlockSpec((None, None, tm, BRANCH_W), lambda bb, i: (P_ZA, bb, i, 0)),
                  pl.BlockSpec((len(POOL_WINDOWS), POOL_GW, POOL_GW), lambda bb, i: (0, 0, 0)),
                  pl.BlockSpec((1, BRANCH_W), lambda bb, i: (0, 0))],
        out_specs=pl.BlockSpec((None, tm, BRANCH_W), lambda bb, i: (bb, i, 0)),
        out_shape=jax.ShapeDtypeStruct((b, t, BRANCH_W), F32),
        scratch_shapes=[pltpu.VMEM((POOL_HALO + tm, BRANCH_W), F32)],
        compiler_params=_cparams("parallel", "parallel"),
    )(proj4, proj4, proj4, pool_w_bf16, pool_scale)


def _pool_sample_kernel(u_ref, prev_ref, z_ref, w_ref, sc_ref, o_ref, *, pos0):
    for g, w in enumerate(POOL_WINDOWS):
        sl = slice(g * POOL_GW, (g + 1) * POOL_GW)
        acc = u_ref[:, sl]
        for k in range(1, w):
            acc = acc + prev_ref[POOL_BUF - k, :, sl]
        cnt = float(min(pos0 + 1, w))
        o_ref[:, sl] = _pool_mix(acc, cnt, u_ref[:, sl], z_ref[:, sl], w_ref, sc_ref, g)


def _pool_sample(proj_s, prev_t, pool_w_bf16, pool_scale, pos0):
    s = proj_s.shape[1]
    return pl.pallas_call(
        functools.partial(_pool_sample_kernel, pos0=pos0),
        grid=(1,),
        in_specs=[pl.BlockSpec((None, s, BRANCH_W), lambda i: (P_UA, 0, 0)),
                  pl.BlockSpec((POOL_BUF, s, BRANCH_W), lambda i: (0, 0, 0)),
                  pl.BlockSpec((None, s, BRANCH_W), lambda i: (P_ZA, 0, 0)),
                  pl.BlockSpec((len(POOL_WINDOWS), POOL_GW, POOL_GW), lambda i: (0, 0, 0)),
                  pl.BlockSpec((1, BRANCH_W), lambda i: (0, 0))],
        out_specs=pl.BlockSpec((s, BRANCH_W), lambda i: (0, 0)),
        out_shape=jax.ShapeDtypeStruct((s, BRANCH_W), F32),
        compiler_params=_cparams("arbitrary"),
    )(proj_s, prev_t, proj_s, pool_w_bf16, pool_scale)


def _s5_discretize(a_re, a_im, b_re, b_im, c_re, c_im, log_dt):
    lam = lax.complex(a_re, a_im)
    step = lam * jnp.exp(log_dt)[:, None]
    a_bar = jnp.exp(step)
    b_bar = ((a_bar - 1.0) / lam)[..., None] * lax.complex(b_re, b_im)
    c_mat = lax.complex(c_re, c_im)
    return step, a_bar, b_bar, c_mat


def _slab_blockdiag(m):
    _, big_r, r, big_c, c = m.shape
    eye = jnp.eye(S5_SLAB_G, dtype=m.dtype)
    m = m.reshape(S5_SLABS, S5_SLAB_G, big_r, r, big_c, c).transpose(0, 2, 1, 3, 4, 5)
    m = m[:, :, :, :, :, None, :] * eye[None, None, :, None, None, :, None]
    return m.reshape(S5_SLABS, big_r * S5_SLAB_G * r, big_c * S5_SLAB_G * c)


def _s5_chunk_operators(a_re, a_im, b_re, b_im, c_re, c_im, log_dt):
    ell = S5_CHUNK
    step, _, b_bar, c_mat = _s5_discretize(a_re, a_im, b_re, b_im, c_re, c_im, log_dt)
    apow = jnp.exp(step[None] * jnp.arange(ell + 1, dtype=F32)[:, None, None])
    ca = c_mat[None] * apow[:, :, None, :]
    ca_re, ca_im = jnp.real(ca), jnp.imag(ca)
    bb_re, bb_im = jnp.real(b_bar), jnp.imag(b_bar)
    kmat = (jnp.einsum('jgop,gpi->gjoi', ca_re[:ell], bb_re, precision=HIGHEST)
            - jnp.einsum('jgop,gpi->gjoi', ca_im[:ell], bb_im, precision=HIGHEST))
    s_idx = jnp.arange(ell)[:, None]
    t_idx = jnp.arange(ell)[None, :]
    lag = jnp.clip(t_idx - s_idx, 0, ell - 1)
    tfull = jnp.where((t_idx >= s_idx)[None, :, :, None, None], kmat[:, lag], 0.0)
    tfull = tfull.transpose(0, 1, 4, 2, 3)
    t_op = _slab_blockdiag(tfull)
    w_c = apow[ell - 1 - jnp.arange(ell)][:, :, :, None] * b_bar[None]
    w_c = w_c.transpose(1, 0, 3, 2)
    w_re = _slab_blockdiag(jnp.real(w_c)[:, :, :, None, :])
    w_im = _slab_blockdiag(jnp.imag(w_c)[:, :, :, None, :])
    w_op = jnp.concatenate([w_re, w_im], axis=2)
    v_re = ca_re[1:].transpose(1, 3, 0, 2)
    v_im = -ca_im[1:].transpose(1, 3, 0, 2)
    v_re = _slab_blockdiag(v_re[:, None])
    v_im = _slab_blockdiag(v_im[:, None])
    v_op = jnp.concatenate([v_re, v_im], axis=1)
    a_l = apow[ell]
    a_r = jnp.real(a_l).reshape(1, S5_GROUPS * S5_STATE)
    a_i = jnp.imag(a_l).reshape(1, S5_GROUPS * S5_STATE)
    return t_op.astype(BF16), w_op.astype(BF16), v_op.astype(BF16), a_r, a_i


def _s5_gather_chunks(u_ref, tr):
    rows = [u_ref[pl.ds(s, tr, stride=S5_CHUNK), :] for s in range(S5_CHUNK)]
    return jnp.concatenate(rows, axis=1).astype(BF16)


def _s5_state_kernel(u_ref, w_ref, o_ref, *, tr):
    o_ref[...] = jnp.dot(_s5_gather_chunks(u_ref, tr), w_ref[...], preferred_element_type=F32)


def _s5_chunk_states(proj, w_op, tr):
    n = proj.shape[1]
    r = n // S5_CHUNK
    return pl.pallas_call(
        functools.partial(_s5_state_kernel, tr=tr),
        grid=(S5_SLABS, r // tr),
        in_specs=[pl.BlockSpec((None, tr * S5_CHUNK, S5_SLAB_W), lambda g, i: (P_UB, i, g)),
                  pl.BlockSpec((None, S5_CHUNK_W, S5_CHUNK_W), lambda g, i: (g, 0, 0))],
        out_specs=pl.BlockSpec((tr, S5_CHUNK_W), lambda g, i: (i, g)),
        out_shape=jax.ShapeDtypeStruct((r, S5_SLABS * S5_CHUNK_W), F32),
        compiler_params=_cparams("parallel", "parallel"),
    )(proj, w_op)


def _s5_scan_kernel(s_ref, ar_ref, ai_ref, hprev_ref, hr_ref, hi_ref, hr_scr, hi_scr, *, cb):
    j = pl.program_id(1)
    sw = S5_SLAB_G * S5_STATE

    @pl.when(j == 0)
    def _():
        hr_scr[...] = jnp.zeros_like(hr_scr)
        hi_scr[...] = jnp.zeros_like(hi_scr)

    for n in range(S5_SLABS):
        re = slice(n * 2 * sw, n * 2 * sw + sw)
        im = slice(n * 2 * sw + sw, (n + 1) * 2 * sw)
        st = slice(n * sw, (n + 1) * sw)
        a_r = ar_ref[:, st]
        a_i = ai_ref[:, st]

        def body(c, carry, re=re, im=im, a_r=a_r, a_i=a_i):
            h_r, h_i = carry
            row = pl.ds(c, 1)
            hprev_ref[row, re] = h_r
            hprev_ref[row, im] = h_i
            n_r = a_r * h_r - a_i * h_i + s_ref[row, re]
            n_i = a_r * h_i + a_i * h_r + s_ref[row, im]
            return n_r, n_i

        h_r, h_i = lax.fori_loop(0, cb, body, (hr_scr[:, st], hi_scr[:, st]))
        hr_scr[:, st] = h_r
        hi_scr[:, st] = h_i

    @pl.when(j == pl.num_programs(1) - 1)
    def _():
        hr_ref[...] = hr_scr[...]
        hi_ref[...] = hi_scr[...]


def _s5_scan(s_loc, a_r, a_i, bsz, cb):
    r, w = s_loc.shape
    nj = r // bsz // cb
    sw = a_r.shape[1]
    vec = pl.BlockSpec((1, sw), lambda bb, j: (0, 0))
    last = pl.BlockSpec((None, 1, sw), lambda bb, j: (bb, 0, 0))
    blk = pl.BlockSpec((cb, w), lambda bb, j: (bb * nj + j, 0))
    return pl.pallas_call(
        functools.partial(_s5_scan_kernel, cb=cb),
        grid=(bsz, nj),
        in_specs=[blk, vec, vec],
        out_specs=[blk, last, last],
        out_shape=[jax.ShapeDtypeStruct((r, w), F32),
                   jax.ShapeDtypeStruct((bsz, 1, sw), F32),
                   jax.ShapeDtypeStruct((bsz, 1, sw), F32)],
        scratch_shapes=[pltpu.VMEM((1, sw), F32), pltpu.VMEM((1, sw), F32)],
        compiler_params=_cparams("parallel", "arbitrary"),
    )(s_loc, a_r, a_i)


def _s5_out_kernel(u_ref, h_ref, t_ref, v_ref, y_ref, *, tr):
    y = (jnp.dot(_s5_gather_chunks(u_ref, tr), t_ref[...], preferred_element_type=F32)
         + jnp.dot(h_ref[...].astype(BF16), v_ref[...], preferred_element_type=F32))
    for s in range(S5_CHUNK):
        y_ref[pl.ds(s, tr, stride=S5_CHUNK), :] = y[:, s * S5_SLAB_W:(s + 1) * S5_SLAB_W]


def _s5_chunk_outputs(proj, h_prev, t_op, v_op, tr):
    n = proj.shape[1]
    r = n // S5_CHUNK
    op_spec = pl.BlockSpec((None, S5_CHUNK_W, S5_CHUNK_W), lambda g, i: (g, 0, 0))
    return pl.pallas_call(
        functools.partial(_s5_out_kernel, tr=tr),
        grid=(S5_SLABS, r // tr),
        in_specs=[pl.BlockSpec((None, tr * S5_CHUNK, S5_SLAB_W), lambda g, i: (P_UB, i, g)),
                  pl.BlockSpec((tr, S5_CHUNK_W), lambda g, i: (i, g)),
                  op_spec, op_spec],
        out_specs=pl.BlockSpec((tr * S5_CHUNK, S5_SLAB_W), lambda g, i: (i, g)),
        out_shape=jax.ShapeDtypeStruct((n, BRANCH_W), F32),
        compiler_params=_cparams("parallel", "parallel"),
    )(proj, h_prev, t_op, v_op)


def _s5_prompt(proj, ops, bsz):
    t_op, w_op, v_op, a_r, a_i = ops
    r = proj.shape[1] // S5_CHUNK
    tr = _tile(r, 256)
    s_loc = _s5_chunk_states(proj, w_op, tr)
    h_prev, h_re, h_im = _s5_scan(s_loc, a_r, a_i, bsz, _tile(r // bsz, 128))
    return _s5_chunk_outputs(proj, h_prev, t_op, v_op, tr), h_re, h_im


def _s5_sample_operators(a_re, a_im, b_re, b_im, c_re, c_im, log_dt):
    _, a_bar, b_bar, c_mat = _s5_discretize(a_re, a_im, b_re, b_im, c_re, c_im, log_dt)
    ns = BRANCH_W // 256
    gs = S5_GROUPS // ns
    eye = jnp.eye(gs, dtype=F32)

    def expand_b(m):
        m = m.reshape(ns, gs, S5_STATE, S5_GROUP).transpose(0, 1, 3, 2)
        m = m[:, :, :, None, :] * eye[None, :, None, :, None]
        return m.reshape(ns, gs * S5_GROUP, gs * S5_STATE)

    def expand_c(m):
        m = m.reshape(ns, gs, S5_GROUP, S5_STATE).transpose(0, 1, 3, 2)
        m = m[:, :, :, None, :] * eye[None, :, None, :, None]
        return m.reshape(ns, gs * S5_STATE, gs * S5_GROUP)

    b_op = jnp.concatenate([expand_b(jnp.real(b_bar)), expand_b(jnp.imag(b_bar))], axis=2)
    c_op = jnp.concatenate([expand_c(jnp.real(c_mat)), expand_c(-jnp.imag(c_mat))], axis=1)
    a_r = jnp.real(a_bar).reshape(1, S5_GROUPS * S5_STATE)
    a_i = jnp.imag(a_bar).reshape(1, S5_GROUPS * S5_STATE)
    return b_op.astype(BF16), c_op.astype(BF16), a_r, a_i


def _s5_sample_kernel(u_ref, hr_ref, hi_ref, ar_ref, ai_ref, b_ref, c_ref, y_ref, or_ref, oi_ref):
    ns = b_ref.shape[0]
    cw = b_ref.shape[1]
    sw = c_ref.shape[1] // 2
    for n in range(ns):
        cs = slice(n * cw, (n + 1) * cw)
        ss = slice(n * sw, (n + 1) * sw)
        bu = jnp.dot(u_ref[:, cs].astype(BF16), b_ref[n], preferred_element_type=F32)
        a_r, a_i = ar_ref[:, ss], ai_ref[:, ss]
        h_r, h_i = hr_ref[:, ss], hi_ref[:, ss]
        n_r = a_r * h_r - a_i * h_i + bu[:, :sw]
        n_i = a_r * h_i + a_i * h_r + bu[:, sw:]
        or_ref[:, ss] = n_r
        oi_ref[:, ss] = n_i
        hcat = jnp.concatenate([n_r, n_i], axis=1).astype(BF16)
        y_ref[:, cs] = jnp.dot(hcat, c_ref[n], preferred_element_type=F32)


def _s5_sample(proj_s, h_re, h_im, ops):
    b_op, c_op, a_r, a_i = ops
    s = proj_s.shape[1]
    sw = h_re.shape[1]
    full = lambda shape: pl.BlockSpec(shape, lambda i: (0,) * len(shape))
    return pl.pallas_call(
        _s5_sample_kernel,
        grid=(1,),
        in_specs=[pl.BlockSpec((None, s, BRANCH_W), lambda i: (P_UB, 0, 0)),
                  full((s, sw)), full((s, sw)), full((1, sw)), full((1, sw)),
                  full(b_op.shape), full(c_op.shape)],
        out_specs=[full((s, BRANCH_W)), full((s, sw)), full((s, sw))],
        out_shape=[jax.ShapeDtypeStruct((s, BRANCH_W), F32),
                   jax.ShapeDtypeStruct((s, sw), F32),
                   jax.ShapeDtypeStruct((s, sw), F32)],
        compiler_params=_cparams("arbitrary"),
    )(proj_s, h_re, h_im, a_r, a_i, b_op, c_op)


def _glu_kernel(y_ref, u_ref, z_ref, d_ref, w_ref, b_ref, o_ref):
    y = y_ref[...] + d_ref[...] * u_ref[...]
    g = jax.nn.gelu(y, approximate=True)
    t = jnp.dot(g.astype(BF16), w_ref[...], preferred_element_type=F32) + b_ref[...]
    o_ref[...] = g * jax.nn.sigmoid(t) * _silu(z_ref[...])


def _glu(y2d, proj, s5_d, w_glu_bf16, b_glu, tm):
    n = y2d.shape[0]
    return pl.pallas_call(
        _glu_kernel,
        grid=(n // tm,),
        in_specs=[pl.BlockSpec((tm, BRANCH_W), lambda i: (i, 0)),
                  pl.BlockSpec((None, tm, BRANCH_W), lambda i: (P_UB, i, 0)),
                  pl.BlockSpec((None, tm, BRANCH_W), lambda i: (P_ZB, i, 0)),
                  pl.BlockSpec((1, BRANCH_W), lambda i: (0, 0)),
                  pl.BlockSpec((BRANCH_W, BRANCH_W), lambda i: (0, 0)),
                  pl.BlockSpec((1, BRANCH_W), lambda i: (0, 0))],
        out_specs=pl.BlockSpec((tm, BRANCH_W), lambda i: (i, 0)),
        out_shape=jax.ShapeDtypeStruct((n, BRANCH_W), F32),
        compiler_params=_cparams("parallel"),
    )(y2d, proj, proj, s5_d, w_glu_bf16, b_glu)


def _neg_suffix_ones(n):
    r = lax.broadcasted_iota(jnp.int32, (n, n), 0)
    c = lax.broadcasted_iota(jnp.int32, (n, n), 1)
    return jnp.where(r >= c, -1.0, 0.0).astype(BF16)


def _attn_kernel(bias_ref, q_ref, kt_ref, vt_ref, z_ref, m_ref, o_ref,
                 kb_scr, vb_scr, qm_scr, carry_scr, acc_scr, *, bq, bk):
    hp = pl.program_id(1)
    i = pl.program_id(2)
    lane = lax.broadcasted_iota(jnp.int32, (bq, PAIR_LANES), 1)

    @pl.when(i == 0)
    def _():
        kb_scr[...] = kt_ref[...].astype(BF16)
        vb_scr[...] = vt_ref[...].astype(BF16)

    q = q_ref[...] * SB_SCALE
    qm_scr[0] = jnp.where(lane < SB_HEAD_DIM, q, 0.0).astype(BF16)
    qm_scr[1] = jnp.where(lane >= SB_HEAD_DIM, q, 0.0).astype(BF16)
    carry_scr[...] = jnp.zeros_like(carry_scr)
    acc_scr[...] = jnp.zeros_like(acc_scr)

    def tile(j, r0, masked):
        rows = bq - r0
        ks = pl.multiple_of(j * bk, bk)
        kb = kb_scr[:, pl.ds(ks, bk)]
        vb = vb_scr[:, pl.ds(ks, bk)]
        if masked:
            qpos = i * bq + r0 + lax.broadcasted_iota(jnp.int32, (rows, bk), 0)
            kpos = j * bk + lax.broadcasted_iota(jnp.int32, (rows, bk), 1)
            causal = kpos < qpos
        for h in range(2):
            z = jnp.dot(qm_scr[h, r0:, :], kb, preferred_element_type=F32)
            z = z + bias_ref[2 * hp + h]
            sp = _softplus(z)
            if masked:
                sp = jnp.where(causal, sp, 0.0)
            nsuf = jnp.dot(sp.astype(BF16), m_ref[...], preferred_element_type=F32)
            carry = carry_scr[h, r0:, :]
            expo = (z + nsuf) + jnp.concatenate([carry] * (bk // PAIR_LANES), axis=1)
            w = jnp.exp2(expo * LOG2E)
            if masked:
                w = jnp.where(causal, w, 0.0)
            acc_scr[h, r0:, :] += lax.dot_general(w.astype(BF16), vb, (((1,), (1,)), ((), ())),
                                                  preferred_element_type=F32)
            carry_scr[h, r0:, :] = carry + jnp.broadcast_to(nsuf[:, 0:1], (rows, PAIR_LANES))

    r = bq // bk
    for jl in reversed(range(r)):
        tile(r * i + jl, jl * bk, True)

    def body(t, c):
        tile(r * i - 1 - t, 0, False)
        return c

    lax.fori_loop(0, r * i, body, 0)

    y = jnp.where(lane < SB_HEAD_DIM, acc_scr[0], acc_scr[1])
    o_ref[...] = y * _silu(z_ref[...])


def _attention_prompt(proj4, kv_bufs, sb_bias, layer, bq, bk):
    _, b, t, _ = proj4.shape
    kv_spec = pl.BlockSpec((None, None, PAIR_LANES, t), lambda bb, hp, i, s: (layer, bb, hp, 0))
    grid_spec = pltpu.PrefetchScalarGridSpec(
        num_scalar_prefetch=1,
        grid=(b, HEAD_PAIRS, t // bq),
        in_specs=[pl.BlockSpec((None, None, bq, PAIR_LANES), lambda bb, hp, i, s: (P_Q, bb, i, hp)),
                  kv_spec, kv_spec,
                  pl.BlockSpec((None, None, bq, PAIR_LANES), lambda bb, hp, i, s: (P_ZC, bb, i, hp)),
                  pl.BlockSpec((bk, bk), lambda bb, hp, i, s: (0, 0))],
        out_specs=pl.BlockSpec((None, bq, PAIR_LANES), lambda bb, hp, i, s: (bb, i, hp)),
        scratch_shapes=[pltpu.VMEM((PAIR_LANES, t), BF16), pltpu.VMEM((PAIR_LANES, t), BF16),
                        pltpu.VMEM((2, bq, PAIR_LANES), BF16),
                        pltpu.VMEM((2, bq, PAIR_LANES), F32), pltpu.VMEM((2, bq, PAIR_LANES), F32)])
    return pl.pallas_call(
        functools.partial(_attn_kernel, bq=bq, bk=bk),
        grid_spec=grid_spec,
        out_shape=jax.ShapeDtypeStruct((b, t, BRANCH_W), F32),
        compiler_params=_cparams("parallel", "parallel", "arbitrary"),
    )(sb_bias, proj4, kv_bufs[0], kv_bufs[1], proj4, _neg_suffix_ones(bk))


def _decode_kernel(pt_ref, qt_ref, z_ref, *refs, pps):
    k_refs = refs[:pps]
    v_refs = refs[pps:2 * pps]
    bias_ref, m_ref, o_ref, qb_scr, carry_scr, acc_scr = refs[2 * pps:]
    s = pl.program_id(1)
    page = k_refs[0].shape[2]

    @pl.when(s == 0)
    def _():
        qt = qt_ref[...] * SB_SCALE
        for h in range(SB_HEADS):
            qb_scr[h] = jnp.broadcast_to(qt[:, h:h + 1], (SB_HEAD_DIM, page))
        carry_scr[...] = jnp.zeros_like(carry_scr)
        acc_scr[...] = jnp.zeros_like(acc_scr)

    for p in range(pps):
        zt = jnp.concatenate([jnp.sum(k_refs[p][h] * qb_scr[h], axis=0, keepdims=True)
                              for h in range(SB_HEADS)], axis=0)
        z = zt + bias_ref[...]
        sp = _softplus(z)
        hi = sp.astype(BF16)
        lo = (sp - hi.astype(F32)).astype(BF16)
        nsuf = (jnp.dot(hi, m_ref[...], preferred_element_type=F32)
                + jnp.dot(lo, m_ref[...], preferred_element_type=F32))
        carry = carry_scr[...]
        w = jnp.exp2(((z + nsuf) + carry) * LOG2E)
        for h in range(SB_HEADS):
            acc_scr[h] += v_refs[p][h] * jnp.broadcast_to(w[h:h + 1, :], (SB_HEAD_DIM, page))
        carry_scr[...] = carry + jnp.broadcast_to(nsuf[:, 0:1], (SB_HEADS, page))

    @pl.when(s == pl.num_programs(1) - 1)
    def _():
        y = jnp.sum(acc_scr[...], axis=-1)
        o_ref[...] = y * _silu(z_ref[...])


def _attention_sample(proj_s, cache_k, cache_v, page_table, bias_col, layer):
    s = proj_s.shape[1]
    n_pages = page_table.shape[1]
    page = cache_k.shape[2]
    pps = next(c for c in (4, 2, 1) if n_pages % c == 0)
    ck = cache_k.transpose(0, 1, 3, 4, 2)
    cv = cache_v.transpose(0, 1, 3, 4, 2)
    heads = lambda part: proj_s[part].reshape(s, SB_HEADS, SB_HEAD_DIM)
    q_t = heads(P_Q).transpose(0, 2, 1)

    def page_spec(j):
        return pl.BlockSpec((None, None, SB_HEADS, SB_HEAD_DIM, page),
                            lambda bb, st, pt: (layer, pt[bb, n_pages - 1 - (st * pps + j)], 0, 0, 0))

    grid_spec = pltpu.PrefetchScalarGridSpec(
        num_scalar_prefetch=1,
        grid=(s, n_pages // pps),
        in_specs=[pl.BlockSpec((None, SB_HEAD_DIM, SB_HEADS), lambda bb, st, pt: (bb, 0, 0)),
                  pl.BlockSpec((None, SB_HEADS, SB_HEAD_DIM), lambda bb, st, pt: (bb, 0, 0))]
                 + [page_spec(j) for j in range(pps)] * 2
                 + [pl.BlockSpec((SB_HEADS, 1), lambda bb, st, pt: (0, 0)),
                    pl.BlockSpec((page, page), lambda bb, st, pt: (0, 0))],
        out_specs=pl.BlockSpec((None, SB_HEADS, SB_HEAD_DIM), lambda bb, st, pt: (bb, 0, 0)),
        scratch_shapes=[pltpu.VMEM((SB_HEADS, SB_HEAD_DIM, page), F32),
                        pltpu.VMEM((SB_HEADS, page), F32),
                        pltpu.VMEM((SB_HEADS, SB_HEAD_DIM, page), F32)])
    out = pl.pallas_call(
        functools.partial(_decode_kernel, pps=pps),
        grid_spec=grid_spec,
        out_shape=jax.ShapeDtypeStruct((s, SB_HEADS, SB_HEAD_DIM), F32),
        compiler_params=_cparams("parallel", "arbitrary"),
    )(page_table, q_t, heads(P_ZC), *([ck] * pps), *([cv] * pps), bias_col, _neg_suffix_ones(page))
    return out.reshape(s, BRANCH_W)


def _merge_kernel(ya_ref, yb_ref, yc_ref, ga_ref, gb_ref, gc_ref, x_ref, wb_ref, wo_ref, g_ref, b_ref, o_ref,
                  *, alpha):
    merged = None
    for n, (y_ref, gate_ref) in enumerate(((ya_ref, ga_ref), (yb_ref, gb_ref), (yc_ref, gc_ref))):
        pb = jnp.dot(y_ref[...].astype(BF16), wb_ref[n], preferred_element_type=F32)
        term = jax.nn.sigmoid(gate_ref[...]) * pb
        merged = term if merged is None else merged + term
    out = jnp.dot(merged.astype(BF16), wo_ref[...], preferred_element_type=F32)
    v = alpha * x_ref[...] + out
    mu = jnp.mean(v, axis=-1, keepdims=True)
    c = v - mu
    var = jnp.mean(c * c, axis=-1, keepdims=True)
    o_ref[...] = c * lax.rsqrt(var + LN_EPS) * g_ref[...] + b_ref[...]


def _merge(ya, yb, yc, proj, x2d, w_branch_bf16, w_out_bf16, ln_g, ln_b, alpha, tm):
    n = x2d.shape[0]
    row = pl.BlockSpec((tm, D_MODEL), lambda i: (i, 0))
    part = lambda k: pl.BlockSpec((None, tm, D_MODEL), lambda i: (k, i, 0))
    vec = pl.BlockSpec((1, D_MODEL), lambda i: (0, 0))
    return pl.pallas_call(
        functools.partial(_merge_kernel, alpha=alpha),
        grid=(n // tm,),
        in_specs=[row, row, row, part(P_GA), part(P_GB), part(P_GC), row,
                  pl.BlockSpec((3, BRANCH_W, D_MODEL), lambda i: (0, 0, 0)),
                  pl.BlockSpec((D_MODEL, D_MODEL), lambda i: (0, 0)),
                  vec, vec],
        out_specs=row,
        out_shape=jax.ShapeDtypeStruct((n, D_MODEL), F32),
        compiler_params=_cparams("parallel"),
    )(ya, yb, yc, proj, proj, proj, x2d, w_branch_bf16, w_out_bf16, ln_g, ln_b)


def _tile(n, pref):
    return pref if n % pref == 0 else n


def _split_w_in(w):
    parts = w.reshape(D_MODEL, N_PROJ, BRANCH_W)
    w_main = parts[:, MAIN_PARTS, :].reshape(D_MODEL, N_MAIN * BRANCH_W).astype(BF16)
    w_kv_t = parts[:, KV_PARTS, :].transpose(1, 2, 0).astype(BF16)
    return w_main, w_kv_t


def _heads_last(kv_t):
    d, b, _, t = kv_t.shape
    return kv_t.reshape(d, b, SB_HEADS, SB_HEAD_DIM, t).transpose(0, 1, 4, 2, 3)


def kernel(x_prompt, x_sample, cache_k, cache_v, state_pool, state_s5_re, state_s5_im, page_table, w_in, pool_w, pool_scale, s5_a_re, s5_a_im, s5_b_re, s5_b_im, s5_c_re, s5_c_im, s5_d, s5_log_dt, w_glu, b_glu, sb_bias, w_branch, w_out, ln_g, ln_b):
    depth = w_in.shape[0]
    alpha = (2.0 * depth) ** 0.25
    bsz, t_len, _ = x_prompt.shape
    dec_b = x_sample.shape[0]
    past_len = page_table.shape[1] * cache_k.shape[2]
    n_prompt = bsz * t_len

    xp = x_prompt.reshape(n_prompt, D_MODEL)
    xs = x_sample.reshape(dec_b, D_MODEL)
    kv_p = kv_s = None
    outs = [[] for _ in range(6)]
    for l in range(depth):
        w_main, w_kv_t = _split_w_in(w_in[l])
        pool_w_b = pool_w[l].astype(BF16)
        w_glu_b = w_glu[l].astype(BF16)
        w_branch_b = w_branch[l].astype(BF16)
        w_out_b = w_out[l].astype(BF16)
        row = lambda a: a[l].reshape(1, -1)
        s5_args = (s5_a_re[l], s5_a_im[l], s5_b_re[l], s5_b_im[l], s5_c_re[l], s5_c_im[l], s5_log_dt[l])

        proj, kv_p = _project(xp, w_main, w_kv_t, kv_p, l, depth, bsz, t_len, _tile(t_len, 1024))
        proj4 = proj.reshape(N_MAIN, bsz, t_len, BRANCH_W)
        ya = _pool_prompt(proj4, pool_w_b, row(pool_scale), _tile(t_len, 512)).reshape(n_prompt, BRANCH_W)
        y5, hp_re, hp_im = _s5_prompt(proj, _s5_chunk_operators(*s5_args), bsz)
        yb = _glu(y5, proj, row(s5_d), w_glu_b, row(b_glu), _tile(n_prompt, 512))
        yc = _attention_prompt(proj4, kv_p, sb_bias[l], l, _tile(t_len, 2048), 256).reshape(n_prompt, BRANCH_W)
        xp = _merge(ya, yb, yc, proj, xp, w_branch_b, w_out_b, row(ln_g), row(ln_b), alpha, _tile(n_prompt, 256))
        outs[0].append(proj4[P_UA, :, t_len - POOL_BUF:, :])
        outs[1].append(hp_re.reshape(bsz, S5_GROUPS, S5_STATE))
        outs[2].append(hp_im.reshape(bsz, S5_GROUPS, S5_STATE))

        proj_s, kv_s = _project(xs, w_main, w_kv_t, kv_s, l, depth, 1, dec_b, dec_b)
        prev_t = state_pool[l].transpose(1, 0, 2)
        ya_s = _pool_sample(proj_s, prev_t, pool_w_b, row(pool_scale), past_len)
        h_re = state_s5_re[l].reshape(dec_b, S5_GROUPS * S5_STATE)
        h_im = state_s5_im[l].reshape(dec_b, S5_GROUPS * S5_STATE)
        y5_s, n_re, n_im = _s5_sample(proj_s, h_re, h_im, _s5_sample_operators(*s5_args))
        yb_s = _glu(y5_s, proj_s, row(s5_d), w_glu_b, row(b_glu), dec_b)
        yc_s = _attention_sample(proj_s, cache_k, cache_v, page_table, sb_bias[l].reshape(SB_HEADS, 1), l)
        xs = _merge(ya_s, yb_s, yc_s, proj_s, xs, w_branch_b, w_out_b, row(ln_g), row(ln_b), alpha, dec_b)
        outs[3].append(jnp.concatenate([prev_t[1:], proj_s[P_UA][None]], axis=0).transpose(1, 0, 2))
        outs[4].append(n_re.reshape(dec_b, S5_GROUPS, S5_STATE))
        outs[5].append(n_im.reshape(dec_b, S5_GROUPS, S5_STATE))

    pool_p, s5r_p, s5i_p, pool_s, s5r_s, s5i_s = (jnp.stack(o) for o in outs)
    k_p, v_p = _heads_last(kv_p[0]), _heads_last(kv_p[1])
    k_s = _heads_last(kv_s[0]).transpose(0, 2, 1, 3, 4)
    v_s = _heads_last(kv_s[1]).transpose(0, 2, 1, 3, 4)
    return (xp.reshape(bsz, t_len, D_MODEL), xs.reshape(dec_b, 1, D_MODEL),
            k_p, v_p, pool_p, s5r_p, s5i_p, k_s, v_s, pool_s, s5r_s, s5i_s)
```

```python
import functools
import math

import jax
import jax.numpy as jnp
from jax import lax
from jax.experimental import pallas as pl
from jax.experimental.pallas import tpu as pltpu

F32 = jnp.float32
BF16 = jnp.bfloat16
HIGHEST = lax.Precision.HIGHEST

D_MODEL = 1024
BRANCH_W = 1024
N_PROJ = 11
MAIN_PARTS = (0, 1, 2, 3, 4, 7, 8, 9, 10)
N_MAIN = len(MAIN_PARTS)
P_UA, P_ZA, P_UB, P_ZB, P_Q, P_ZC, P_GA, P_GB, P_GC = range(N_MAIN)
KV_PARTS = (5, 6)
LOG2E = 1.4426950408889634
POOL_WINDOWS = (2, 4, 8, 16)
POOL_GW = BRANCH_W // len(POOL_WINDOWS)
POOL_BUF = max(POOL_WINDOWS) - 1
POOL_HALO = 16
S5_GROUP = 16
S5_GROUPS = BRANCH_W // S5_GROUP
S5_STATE = 64
S5_CHUNK = 8
S5_SLAB_W = 128
S5_SLAB_G = S5_SLAB_W // S5_GROUP
S5_SLABS = BRANCH_W // S5_SLAB_W
S5_CHUNK_W = S5_CHUNK * S5_SLAB_W
SB_HEADS = 16
SB_HEAD_DIM = BRANCH_W // SB_HEADS
SB_SCALE = 1.0 / math.sqrt(SB_HEAD_DIM)
HEAD_PAIRS = SB_HEADS // 2
PAIR_LANES = 2 * SB_HEAD_DIM
LN_EPS = 1e-5
VMEM_LIMIT = 56 * 1024 * 1024


def _cparams(*sem):
    return pltpu.CompilerParams(dimension_semantics=sem, vmem_limit_bytes=VMEM_LIMIT)


def _silu(z):
    return z * jax.nn.sigmoid(z)


def _softplus(z):
    return jnp.maximum(z, 0.0) + jnp.log(1.0 + jnp.exp2(jnp.abs(z) * (-LOG2E)))


def _proj_kernel(*refs):
    x_ref, wm_ref, wt_ref = refs[:3]
    o_ref, kt_ref, vt_ref, xb_ref = refs[-4:]
    j = pl.program_id(1)

    @pl.when(j == 0)
    def _():
        xb_ref[...] = x_ref[...].astype(BF16)

    @pl.when(j < N_MAIN)
    def _():
        o_ref[...] = jnp.dot(xb_ref[...], wm_ref[...], preferred_element_type=F32)

    for part, t_ref in enumerate((kt_ref, vt_ref)):
        @pl.when(j == N_MAIN + part)
        def _():
            t_ref[...] = lax.dot_general(wt_ref[...], xb_ref[...], (((1,), (1,)), ((), ())),
                                         preferred_element_type=F32)


def _project(x2d, w_main, w_kv_t, kv_bufs, layer, depth, bsz, t_len, tm):
    n = x2d.shape[0]
    tpb = t_len // tm
    n_kv = len(KV_PARTS)
    main_j = lambda j: jnp.minimum(j, N_MAIN - 1)
    kv_j = lambda j: jnp.maximum(j - N_MAIN, 0)
    in_specs = [pl.BlockSpec((tm, D_MODEL), lambda i, j: (i, 0)),
                pl.BlockSpec((D_MODEL, BRANCH_W), lambda i, j: (0, main_j(j))),
                pl.BlockSpec((None, BRANCH_W, D_MODEL), lambda i, j: (kv_j(j), 0, 0))]
    args = [x2d, w_main, w_kv_t]
    aliases = {}
    if kv_bufs is not None:
        in_specs += [pl.BlockSpec(memory_space=pl.ANY)] * n_kv
        args += list(kv_bufs)
        aliases = {3: 1, 4: 2}
    kv_spec = pl.BlockSpec((None, None, BRANCH_W, tm), lambda i, j: (layer, i // tpb, 0, i % tpb))
    kv_shape = jax.ShapeDtypeStruct((depth, bsz, BRANCH_W, t_len), F32)
    main, k_buf, v_buf = pl.pallas_call(
        _proj_kernel,
        grid=(n // tm, N_MAIN + n_kv),
        in_specs=in_specs,
        out_specs=[pl.BlockSpec((None, tm, BRANCH_W), lambda i, j: (main_j(j), i, 0)), kv_spec, kv_spec],
        out_shape=[jax.ShapeDtypeStruct((N_MAIN, n, BRANCH_W), F32), kv_shape, kv_shape],
        scratch_shapes=[pltpu.VMEM((tm, D_MODEL), BF16)],
        input_output_aliases=aliases,
        compiler_params=_cparams("parallel", "arbitrary"),
    )(*args)
    return main, (k_buf, v_buf)


def _pool_mix(acc, cnt, u, z, w_ref, sc_ref, g):
    sl = slice(g * POOL_GW, (g + 1) * POOL_GW)
    pooled = acc / cnt - u
    mixed = jnp.dot(pooled.astype(BF16), w_ref[g], preferred_element_type=F32)
    return mixed * sc_ref[:, sl] * _silu(z)


def _pool_prompt_kernel(u_ref, halo_ref, z_ref, w_ref, sc_ref, o_ref, ext_ref, *, tm):
    i = pl.program_id(1)
    ext_ref[0:POOL_HALO, :] = jnp.where(i == 0, 0.0, halo_ref[...])
    ext_ref[POOL_HALO:POOL_HALO + tm, :] = u_ref[...]
    pos = i * tm + lax.broadcasted_iota(jnp.int32, (tm, 1), 0)
    for g, w in enumerate(POOL_WINDOWS):
        sl = slice(g * POOL_GW, (g + 1) * POOL_GW)
        acc = ext_ref[POOL_HALO:POOL_HALO + tm, sl]
        for k in range(1, w):
            acc = acc + ext_ref[POOL_HALO - k:POOL_HALO - k + tm, sl]
        cnt = jnp.minimum(pos + 1, w).astype(F32)
        o_ref[:, sl] = _pool_mix(acc, cnt, u_ref[:, sl], z_ref[:, sl], w_ref, sc_ref, g)


def _pool_prompt(proj4, pool_w_bf16, pool_scale, tm):
    _, b, t, _ = proj4.shape
    hb = tm // POOL_HALO
    return pl.pallas_call(
        functools.partial(_pool_prompt_kernel, tm=tm),
        grid=(b, t // tm),
        in_specs=[pl.BlockSpec((None, None, tm, BRANCH_W), lambda bb, i: (P_UA, bb, i, 0)),
                  pl.BlockSpec((None, None, POOL_HALO, BRANCH_W),
                               lambda bb, i: (P_UA, bb, jnp.maximum(i * hb - 1, 0), 0)),
                  pl.BlockSpec((None, None, tm, BRANCH_W), lambda bb, i: (P_ZA, bb, i, 0)),
                  pl.BlockSpec((len(POOL_WINDOWS), POOL_GW, POOL_GW), lambda bb, i: (0, 0, 0)),
                  pl.BlockSpec((1, BRANCH_W), lambda bb, i: (0, 0))],
        out_specs=pl.BlockSpec((None, tm, BRANCH_W), lambda bb, i: (bb, i, 0)),
        out_shape=jax.ShapeDtypeStruct((b, t, BRANCH_W), F32),
        scratch_shapes=[pltpu.VMEM((POOL_HALO + tm, BRANCH_W), F32)],
        compiler_params=_cparams("parallel", "parallel"),
    )(proj4, proj4, proj4, pool_w_bf16, pool_scale)


def _pool_sample_kernel(u_ref, prev_ref, z_ref, w_ref, sc_ref, o_ref, *, pos0):
    for g, w in enumerate(POOL_WINDOWS):
        sl = slice(g * POOL_GW, (g + 1) * POOL_GW)
        acc = u_ref[:, sl]
        for k in range(1, w):
            acc = acc + prev_ref[POOL_BUF - k, :, sl]
        cnt = float(min(pos0 + 1, w))
        o_ref[:, sl] = _pool_mix(acc, cnt, u_ref[:, sl], z_ref[:, sl], w_ref, sc_ref, g)


def _pool_sample(proj_s, prev_t, pool_w_bf16, pool_scale, pos0):
    s = proj_s.shape[1]
    return pl.pallas_call(
        functools.partial(_pool_sample_kernel, pos0=pos0),
        grid=(1,),
        in_specs=[pl.BlockSpec((None, s, BRANCH_W), lambda i: (P_UA, 0, 0)),
                  pl.BlockSpec((POOL_BUF, s, BRANCH_W), lambda i: (0, 0, 0)),
                  pl.BlockSpec((None, s, BRANCH_W), lambda i: (P_ZA, 0, 0)),
                  pl.BlockSpec((len(POOL_WINDOWS), POOL_GW, POOL_GW), lambda i: (0, 0, 0)),
                  pl.BlockSpec((1, BRANCH_W), lambda i: (0, 0))],
        out_specs=pl.BlockSpec((s, BRANCH_W), lambda i: (0, 0)),
        out_shape=jax.ShapeDtypeStruct((s, BRANCH_W), F32),
        compiler_params=_cparams("arbitrary"),
    )(proj_s, prev_t, proj_s, pool_w_bf16, pool_scale)


def _s5_discretize(a_re, a_im, b_re, b_im, c_re, c_im, log_dt):
    lam = lax.complex(a_re, a_im)
    step = lam * jnp.exp(log_dt)[:, None]
    a_bar = jnp.exp(step)
    b_bar = ((a_bar - 1.0) / lam)[..., None] * lax.complex(b_re, b_im)
    c_mat = lax.complex(c_re, c_im)
    return step, a_bar, b_bar, c_mat


def _same_group(rows, row_unit, cols, col_unit, n_groups):
    r = lax.broadcasted_iota(jnp.int32, (rows, cols), 0) // row_unit % n_groups
    c = lax.broadcasted_iota(jnp.int32, (rows, cols), 1) // col_unit % n_groups
    return r == c


def _spread_tokens(m):
    k, n = S5_CHUNK * S5_GROUP, S5_CHUNK_W
    r = lax.broadcasted_iota(jnp.int32, (k, n), 0)
    c = lax.broadcasted_iota(jnp.int32, (k, n), 1)
    sel = ((r // S5_GROUP == c // S5_SLAB_W) & (r % S5_GROUP == c % S5_GROUP)).astype(F32)
    return jnp.einsum('...rk,kc->...rc', m, sel, precision=HIGHEST)


def _s5_chunk_operators(a_re, a_im, b_re, b_im, c_re, c_im, log_dt):
    ell = S5_CHUNK
    step, _, b_bar, c_mat = _s5_discretize(a_re, a_im, b_re, b_im, c_re, c_im, log_dt)
    apow = jnp.exp(step[None] * jnp.arange(ell + 1, dtype=F32)[:, None, None])
    ca = c_mat[None] * apow[:, :, None, :]
    ca_re, ca_im = jnp.real(ca), jnp.imag(ca)
    bb_re, bb_im = jnp.real(b_bar), jnp.imag(b_bar)
    kmat = (jnp.einsum('jgop,gpi->gjoi', ca_re[:ell], bb_re, precision=HIGHEST)
            - jnp.einsum('jgop,gpi->gjoi', ca_im[:ell], bb_im, precision=HIGHEST))
    s_idx = jnp.arange(ell)[:, None]
    t_idx = jnp.arange(ell)[None, :]
    lag = jnp.clip(t_idx - s_idx, 0, ell - 1)
    tfull = jnp.where((t_idx >= s_idx)[None, :, :, None, None], kmat[:, lag], 0.0)
    ns, ng, sw = S5_SLABS, S5_SLAB_G, S5_SLAB_G * S5_STATE
    t_c = tfull.reshape(ns, ng, ell, ell, S5_GROUP, S5_GROUP).transpose(0, 2, 1, 5, 3, 4)
    t_c = t_c.reshape(ns, S5_CHUNK_W, ell * S5_GROUP)
    t_op = jnp.where(_same_group(S5_CHUNK_W, S5_GROUP, S5_CHUNK_W, S5_GROUP, ng), _spread_tokens(t_c), 0.0)
    w_c = apow[ell - 1 - jnp.arange(ell)][:, :, :, None] * b_bar[None]
    w_c = w_c.reshape(ell, ns, ng, S5_STATE, S5_GROUP).transpose(1, 0, 2, 4, 3)
    w_c = w_c.reshape(ns, S5_CHUNK_W, S5_STATE)
    w_mask = _same_group(S5_CHUNK_W, S5_GROUP, sw, S5_STATE, ng)
    w_half = lambda m: jnp.where(w_mask, jnp.tile(m, (1, 1, ng)), 0.0)
    w_op = jnp.concatenate([w_half(jnp.real(w_c)), w_half(jnp.imag(w_c))], axis=2)
    v_mask = _same_group(sw, S5_STATE, S5_CHUNK_W, S5_GROUP, ng)

    def v_half(m):
        m = m.transpose(1, 3, 0, 2).reshape(ns, sw, ell * S5_GROUP)
        return jnp.where(v_mask, _spread_tokens(m), 0.0)

    v_op = jnp.concatenate([v_half(ca_re[1:]), v_half(-ca_im[1:])], axis=1)
    a_l = apow[ell]
    a_r = jnp.real(a_l).reshape(1, S5_GROUPS * S5_STATE)
    a_i = jnp.imag(a_l).reshape(1, S5_GROUPS * S5_STATE)
    return t_op.astype(BF16), w_op.astype(BF16), v_op.astype(BF16), a_r, a_i


def _s5_gather_chunks(u_ref, tr):
    rows = [u_ref[pl.ds(s, tr, stride=S5_CHUNK), :] for s in range(S5_CHUNK)]
    return jnp.concatenate(rows, axis=1).astype(BF16)


def _s5_state_kernel(u_ref, w_ref, o_ref, *, tr):
    o_ref[...] = jnp.dot(_s5_gather_chunks(u_ref, tr), w_ref[...], preferred_element_type=F32)


def _s5_chunk_states(proj, w_op, tr):
    n = proj.shape[1]
    r = n // S5_CHUNK
    return pl.pallas_call(
        functools.partial(_s5_state_kernel, tr=tr),
        grid=(S5_SLABS, r // tr),
        in_specs=[pl.BlockSpec((None, tr * S5_CHUNK, S5_SLAB_W), lambda g, i: (P_UB, i, g)),
                  pl.BlockSpec((None, S5_CHUNK_W, S5_CHUNK_W), lambda g, i: (g, 0, 0))],
        out_specs=pl.BlockSpec((tr, S5_CHUNK_W), lambda g, i: (i, g)),
        out_shape=jax.ShapeDtypeStruct((r, S5_SLABS * S5_CHUNK_W), F32),
        compiler_params=_cparams("parallel", "parallel"),
    )(proj, w_op)


def _s5_scan_kernel(s_ref, ar_ref, ai_ref, hprev_ref, hr_ref, hi_ref, hr_scr, hi_scr, *, cb):
    j = pl.program_id(1)
    sw = S5_SLAB_G * S5_STATE

    @pl.when(j == 0)
    def _():
        hr_scr[...] = jnp.zeros_like(hr_scr)
        hi_scr[...] = jnp.zeros_like(hi_scr)

    for n in range(S5_SLABS):
        re = slice(n * 2 * sw, n * 2 * sw + sw)
        im = slice(n * 2 * sw + sw, (n + 1) * 2 * sw)
        st = slice(n * sw, (n + 1) * sw)
        a_r = ar_ref[:, st]
        a_i = ai_ref[:, st]

        def body(c, carry, re=re, im=im, a_r=a_r, a_i=a_i):
            h_r, h_i = carry
            row = pl.ds(c, 1)
            hprev_ref[row, re] = h_r
            hprev_ref[row, im] = h_i
            n_r = a_r * h_r - a_i * h_i + s_ref[row, re]
            n_i = a_r * h_i + a_i * h_r + s_ref[row, im]
            return n_r, n_i

        h_r, h_i = lax.fori_loop(0, cb, body, (hr_scr[:, st], hi_scr[:, st]))
        hr_scr[:, st] = h_r
        hi_scr[:, st] = h_i

    @pl.when(j == pl.num_programs(1) - 1)
    def _():
        hr_ref[...] = hr_scr[...]
        hi_ref[...] = hi_scr[...]


def _s5_scan(s_loc, a_r, a_i, bsz, cb):
    r, w = s_loc.shape
    nj = r // bsz // cb
    sw = a_r.shape[1]
    vec = pl.BlockSpec((1, sw), lambda bb, j: (0, 0))
    last = pl.BlockSpec((None, 1, sw), lambda bb, j: (bb, 0, 0))
    blk = pl.BlockSpec((cb, w), lambda bb, j: (bb * nj + j, 0))
    return pl.pallas_call(
        functools.partial(_s5_scan_kernel, cb=cb),
        grid=(bsz, nj),
        in_specs=[blk, vec, vec],
        out_specs=[blk, last, last],
        out_shape=[jax.ShapeDtypeStruct((r, w), F32),
                   jax.ShapeDtypeStruct((bsz, 1, sw), F32),
                   jax.ShapeDtypeStruct((bsz, 1, sw), F32)],
        scratch_shapes=[pltpu.VMEM((1, sw), F32), pltpu.VMEM((1, sw), F32)],
        compiler_params=_cparams("parallel", "arbitrary"),
    )(s_loc, a_r, a_i)


def _s5_out_kernel(u_ref, h_ref, t_ref, v_ref, y_ref, *, tr):
    y = (jnp.dot(_s5_gather_chunks(u_ref, tr), t_ref[...], preferred_element_type=F32)
         + jnp.dot(h_ref[...].astype(BF16), v_ref[...], preferred_element_type=F32))
    for s in range(S5_CHUNK):
        y_ref[pl.ds(s, tr, stride=S5_CHUNK), :] = y[:, s * S5_SLAB_W:(s + 1) * S5_SLAB_W]


def _s5_chunk_outputs(proj, h_prev, t_op, v_op, tr):
    n = proj.shape[1]
    r = n // S5_CHUNK
    op_spec = pl.BlockSpec((None, S5_CHUNK_W, S5_CHUNK_W), lambda g, i: (g, 0, 0))
    return pl.pallas_call(
        functools.partial(_s5_out_kernel, tr=tr),
        grid=(S5_SLABS, r // tr),
        in_specs=[pl.BlockSpec((None, tr * S5_CHUNK, S5_SLAB_W), lambda g, i: (P_UB, i, g)),
                  pl.BlockSpec((tr, S5_CHUNK_W), lambda g, i: (i, g)),
                  op_spec, op_spec],
        out_specs=pl.BlockSpec((tr * S5_CHUNK, S5_SLAB_W), lambda g, i: (i, g)),
        out_shape=jax.ShapeDtypeStruct((n, BRANCH_W), F32),
        compiler_params=_cparams("parallel", "parallel"),
    )(proj, h_prev, t_op, v_op)


def _s5_prompt(proj, ops, bsz):
    t_op, w_op, v_op, a_r, a_i = ops
    r = proj.shape[1] // S5_CHUNK
    tr = _tile(r, 256)
    s_loc = _s5_chunk_states(proj, w_op, tr)
    h_prev, h_re, h_im = _s5_scan(s_loc, a_r, a_i, bsz, _tile(r // bsz, 128))
    return _s5_chunk_outputs(proj, h_prev, t_op, v_op, tr), h_re, h_im


def _s5_sample_operators(a_re, a_im, b_re, b_im, c_re, c_im, log_dt):
    _, a_bar, b_bar, c_mat = _s5_discretize(a_re, a_im, b_re, b_im, c_re, c_im, log_dt)
    ns = BRANCH_W // 256
    gs = S5_GROUPS // ns

    def expand_b(m):
        m = m.reshape(ns, gs, S5_STATE, S5_GROUP).transpose(0, 1, 3, 2)
        m = jnp.tile(m.reshape(ns, gs * S5_GROUP, S5_STATE), (1, 1, gs))
        return jnp.where(_same_group(m.shape[1], S5_GROUP, m.shape[2], S5_STATE, gs), m, 0.0)

    def expand_c(m):
        m = m.reshape(ns, gs, S5_GROUP, S5_STATE).transpose(0, 1, 3, 2)
        m = jnp.tile(m.reshape(ns, gs * S5_STATE, S5_GROUP), (1, 1, gs))
        return jnp.where(_same_group(m.shape[1], S5_STATE, m.shape[2], S5_GROUP, gs), m, 0.0)

    b_op = jnp.concatenate([expand_b(jnp.real(b_bar)), expand_b(jnp.imag(b_bar))], axis=2)
    c_op = jnp.concatenate([expand_c(jnp.real(c_mat)), expand_c(-jnp.imag(c_mat))], axis=1)
    a_r = jnp.real(a_bar).reshape(1, S5_GROUPS * S5_STATE)
    a_i = jnp.imag(a_bar).reshape(1, S5_GROUPS * S5_STATE)
    return b_op.astype(BF16), c_op.astype(BF16), a_r, a_i


def _s5_sample_kernel(u_ref, hr_ref, hi_ref, ar_ref, ai_ref, b_ref, c_ref, y_ref, or_ref, oi_ref):
    ns = b_ref.shape[0]
    cw = b_ref.shape[1]
    sw = c_ref.shape[1] // 2
    for n in range(ns):
        cs = slice(n * cw, (n + 1) * cw)
        ss = slice(n * sw, (n + 1) * sw)
        bu = jnp.dot(u_ref[:, cs].astype(BF16), b_ref[n], preferred_element_type=F32)
        a_r, a_i = ar_ref[:, ss], ai_ref[:, ss]
        h_r, h_i = hr_ref[:, ss], hi_ref[:, ss]
        n_r = a_r * h_r - a_i * h_i + bu[:, :sw]
        n_i = a_r * h_i + a_i * h_r + bu[:, sw:]
        or_ref[:, ss] = n_r
        oi_ref[:, ss] = n_i
        hcat = jnp.concatenate([n_r, n_i], axis=1).astype(BF16)
        y_ref[:, cs] = jnp.dot(hcat, c_ref[n], preferred_element_type=F32)


def _s5_sample(proj_s, h_re, h_im, ops):
    b_op, c_op, a_r, a_i = ops
    s = proj_s.shape[1]
    sw = h_re.shape[1]
    full = lambda shape: pl.BlockSpec(shape, lambda i: (0,) * len(shape))
    return pl.pallas_call(
        _s5_sample_kernel,
        grid=(1,),
        in_specs=[pl.BlockSpec((None, s, BRANCH_W), lambda i: (P_UB, 0, 0)),
                  full((s, sw)), full((s, sw)), full((1, sw)), full((1, sw)),
                  full(b_op.shape), full(c_op.shape)],
        out_specs=[full((s, BRANCH_W)), full((s, sw)), full((s, sw))],
        out_shape=[jax.ShapeDtypeStruct((s, BRANCH_W), F32),
                   jax.ShapeDtypeStruct((s, sw), F32),
                   jax.ShapeDtypeStruct((s, sw), F32)],
        compiler_params=_cparams("arbitrary"),
    )(proj_s, h_re, h_im, a_r, a_i, b_op, c_op)


def _glu_kernel(y_ref, u_ref, z_ref, d_ref, w_ref, b_ref, o_ref):
    y = y_ref[...] + d_ref[...] * u_ref[...]
    g = jax.nn.gelu(y, approximate=True)
    t = jnp.dot(g.astype(BF16), w_ref[...], preferred_element_type=F32) + b_ref[...]
    o_ref[...] = g * jax.nn.sigmoid(t) * _silu(z_ref[...])


def _glu(y2d, proj, s5_d, w_glu_bf16, b_glu, tm):
    n = y2d.shape[0]
    return pl.pallas_call(
        _glu_kernel,
        grid=(n // tm,),
        in_specs=[pl.BlockSpec((tm, BRANCH_W), lambda i: (i, 0)),
                  pl.BlockSpec((None, tm, BRANCH_W), lambda i: (P_UB, i, 0)),
                  pl.BlockSpec((None, tm, BRANCH_W), lambda i: (P_ZB, i, 0)),
                  pl.BlockSpec((1, BRANCH_W), lambda i: (0, 0)),
                  pl.BlockSpec((BRANCH_W, BRANCH_W), lambda i: (0, 0)),
                  pl.BlockSpec((1, BRANCH_W), lambda i: (0, 0))],
        out_specs=pl.BlockSpec((tm, BRANCH_W), lambda i: (i, 0)),
        out_shape=jax.ShapeDtypeStruct((n, BRANCH_W), F32),
        compiler_params=_cparams("parallel"),
    )(y2d, proj, proj, s5_d, w_glu_bf16, b_glu)


def _neg_suffix_ones(n):
    r = lax.broadcasted_iota(jnp.int32, (n, n), 0)
    c = lax.broadcasted_iota(jnp.int32, (n, n), 1)
    return jnp.where(r >= c, -1.0, 0.0).astype(BF16)


def _attn_kernel(bias_ref, q_ref, kt_ref, vt_ref, z_ref, m_ref, o_ref,
                 kb_scr, vb_scr, qm_scr, carry_scr, acc_scr, *, bq, bk):
    hp = pl.program_id(1)
    i = pl.program_id(2)
    lane = lax.broadcasted_iota(jnp.int32, (bq, PAIR_LANES), 1)

    @pl.when(i == 0)
    def _():
        kb_scr[...] = kt_ref[...].astype(BF16)
        vb_scr[...] = vt_ref[...].astype(BF16)

    q = q_ref[...] * SB_SCALE
    qm_scr[0] = jnp.where(lane < SB_HEAD_DIM, q, 0.0).astype(BF16)
    qm_scr[1] = jnp.where(lane >= SB_HEAD_DIM, q, 0.0).astype(BF16)
    carry_scr[...] = jnp.zeros_like(carry_scr)
    acc_scr[...] = jnp.zeros_like(acc_scr)

    def tile(j, r0, masked):
        rows = bq - r0
        ks = pl.multiple_of(j * bk, bk)
        kb = kb_scr[:, pl.ds(ks, bk)]
        vb = vb_scr[:, pl.ds(ks, bk)]
        if masked:
            qpos = i * bq + r0 + lax.broadcasted_iota(jnp.int32, (rows, bk), 0)
            kpos = j * bk + lax.broadcasted_iota(jnp.int32, (rows, bk), 1)
            causal = kpos < qpos
        for h in range(2):
            z = jnp.dot(qm_scr[h, r0:, :], kb, preferred_element_type=F32)
            z = z + bias_ref[2 * hp + h]
            sp = _softplus(z)
            if masked:
                sp = jnp.where(causal, sp, 0.0)
            nsuf = jnp.dot(sp.astype(BF16), m_ref[...], preferred_element_type=F32)
            carry = carry_scr[h, r0:, :]
            expo = (z + nsuf) + jnp.concatenate([carry] * (bk // PAIR_LANES), axis=1)
            w = jnp.exp2(expo * LOG2E)
            if masked:
                w = jnp.where(causal, w, 0.0)
            acc_scr[h, r0:, :] += lax.dot_general(w.astype(BF16), vb, (((1,), (1,)), ((), ())),
                                                  preferred_element_type=F32)
            carry_scr[h, r0:, :] = carry + jnp.broadcast_to(nsuf[:, 0:1], (rows, PAIR_LANES))

    r = bq // bk
    for jl in reversed(range(r)):
        tile(r * i + jl, jl * bk, True)

    def body(t, c):
        tile(r * i - 1 - t, 0, False)
        return c

    lax.fori_loop(0, r * i, body, 0)

    y = jnp.where(lane < SB_HEAD_DIM, acc_scr[0], acc_scr[1])
    o_ref[...] = y * _silu(z_ref[...])


def _attention_prompt(proj4, kv_bufs, sb_bias, layer, bq, bk):
    _, b, t, _ = proj4.shape
    kv_spec = pl.BlockSpec((None, None, PAIR_LANES, t), lambda bb, hp, i, s: (layer, bb, hp, 0))
    grid_spec = pltpu.PrefetchScalarGridSpec(
        num_scalar_prefetch=1,
        grid=(b, HEAD_PAIRS, t // bq),
        in_specs=[pl.BlockSpec((None, None, bq, PAIR_LANES), lambda bb, hp, i, s: (P_Q, bb, i, hp)),
                  kv_spec, kv_spec,
                  pl.BlockSpec((None, None, bq, PAIR_LANES), lambda bb, hp, i, s: (P_ZC, bb, i, hp)),
                  pl.BlockSpec((bk, bk), lambda bb, hp, i, s: (0, 0))],
        out_specs=pl.BlockSpec((None, bq, PAIR_LANES), lambda bb, hp, i, s: (bb, i, hp)),
        scratch_shapes=[pltpu.VMEM((PAIR_LANES, t), BF16), pltpu.VMEM((PAIR_LANES, t), BF16),
                        pltpu.VMEM((2, bq, PAIR_LANES), BF16),
                        pltpu.VMEM((2, bq, PAIR_LANES), F32), pltpu.VMEM((2, bq, PAIR_LANES), F32)])
    return pl.pallas_call(
        functools.partial(_attn_kernel, bq=bq, bk=bk),
        grid_spec=grid_spec,
        out_shape=jax.ShapeDtypeStruct((b, t, BRANCH_W), F32),
        compiler_params=_cparams("parallel", "parallel", "arbitrary"),
    )(sb_bias, proj4, kv_bufs[0], kv_bufs[1], proj4, _neg_suffix_ones(bk))


def _decode_kernel(pt_ref, qt_ref, z_ref, *refs, pps):
    k_refs = refs[:pps]
    v_refs = refs[pps:2 * pps]
    bias_ref, m_ref, o_ref, qb_scr, carry_scr, acc_scr = refs[2 * pps:]
    s = pl.program_id(1)
    page = k_refs[0].shape[2]

    @pl.when(s == 0)
    def _():
        qt = qt_ref[...] * SB_SCALE
        for h in range(SB_HEADS):
            qb_scr[h] = jnp.broadcast_to(qt[:, h:h + 1], (SB_HEAD_DIM, page))
        carry_scr[...] = jnp.zeros_like(carry_scr)
        acc_scr[...] = jnp.zeros_like(acc_scr)

    z = jnp.concatenate([jnp.sum(k_refs[p][h] * qb_scr[h], axis=0, keepdims=True)
                         for p in range(pps) for h in range(SB_HEADS)], axis=0) + bias_ref[...]
    sp = _softplus(z)
    hi = sp.astype(BF16)
    lo = (sp - hi.astype(F32)).astype(BF16)
    nsuf = (jnp.dot(hi, m_ref[...], preferred_element_type=F32)
            + jnp.dot(lo, m_ref[...], preferred_element_type=F32))
    carry = carry_scr[...]
    ws = []
    for p in range(pps):
        rows = slice(p * SB_HEADS, (p + 1) * SB_HEADS)
        ws.append(jnp.exp2(((z[rows] + nsuf[rows]) + carry) * LOG2E))
        carry = carry + jnp.broadcast_to(nsuf[rows][:, 0:1], (SB_HEADS, page))
    carry_scr[...] = carry
    for h in range(SB_HEADS):
        acc = acc_scr[h]
        for p in range(pps):
            acc = acc + v_refs[p][h] * jnp.broadcast_to(ws[p][h:h + 1, :], (SB_HEAD_DIM, page))
        acc_scr[h] = acc

    @pl.when(s == pl.num_programs(1) - 1)
    def _():
        y = jnp.sum(acc_scr[...], axis=-1)
        o_ref[...] = y * _silu(z_ref[...])


def _attention_sample(proj_s, cache_k, cache_v, page_table, bias_col, layer):
    s = proj_s.shape[1]
    n_pages = page_table.shape[1]
    page = cache_k.shape[2]
    pps = next(c for c in (8, 4, 2, 1) if n_pages % c == 0)
    ck = cache_k.transpose(0, 1, 3, 4, 2)
    cv = cache_v.transpose(0, 1, 3, 4, 2)
    heads = lambda part: proj_s[part].reshape(s, SB_HEADS, SB_HEAD_DIM)
    q_t = heads(P_Q).transpose(0, 2, 1)

    def page_spec(j):
        return pl.BlockSpec((None, None, SB_HEADS, SB_HEAD_DIM, page),
                            lambda bb, st, pt: (layer, pt[bb, n_pages - 1 - (st * pps + j)], 0, 0, 0))

    grid_spec = pltpu.PrefetchScalarGridSpec(
        num_scalar_prefetch=1,
        grid=(s, n_pages // pps),
        in_specs=[pl.BlockSpec((None, SB_HEAD_DIM, SB_HEADS), lambda bb, st, pt: (bb, 0, 0)),
                  pl.BlockSpec((None, SB_HEADS, SB_HEAD_DIM), lambda bb, st, pt: (bb, 0, 0))]
                 + [page_spec(j) for j in range(pps)] * 2
                 + [pl.BlockSpec((pps * SB_HEADS, 1), lambda bb, st, pt: (0, 0)),
                    pl.BlockSpec((page, page), lambda bb, st, pt: (0, 0))],
        out_specs=pl.BlockSpec((None, SB_HEADS, SB_HEAD_DIM), lambda bb, st, pt: (bb, 0, 0)),
        scratch_shapes=[pltpu.VMEM((SB_HEADS, SB_HEAD_DIM, page), F32),
                        pltpu.VMEM((SB_HEADS, page), F32),
                        pltpu.VMEM((SB_HEADS, SB_HEAD_DIM, page), F32)])
    out = pl.pallas_call(
        functools.partial(_decode_kernel, pps=pps),
        grid_spec=grid_spec,
        out_shape=jax.ShapeDtypeStruct((s, SB_HEADS, SB_HEAD_DIM), F32),
        compiler_params=_cparams("parallel", "arbitrary"),
    )(page_table, q_t, heads(P_ZC), *([ck] * pps), *([cv] * pps), jnp.tile(bias_col, (pps, 1)), _neg_suffix_ones(page))
    return out.reshape(s, BRANCH_W)


def _merge_kernel(ya_ref, yb_ref, yc_ref, ga_ref, gb_ref, gc_ref, x_ref, wb_ref, wo_ref, g_ref, b_ref, o_ref,
                  *, alpha):
    merged = None
    for n, (y_ref, gate_ref) in enumerate(((ya_ref, ga_ref), (yb_ref, gb_ref), (yc_ref, gc_ref))):
        pb = jnp.dot(y_ref[...].astype(BF16), wb_ref[n], preferred_element_type=F32)
        term = jax.nn.sigmoid(gate_ref[...]) * pb
        merged = term if merged is None else merged + term
    out = jnp.dot(merged.astype(BF16), wo_ref[...], preferred_element_type=F32)
    v = alpha * x_ref[...] + out
    mu = jnp.mean(v, axis=-1, keepdims=True)
    c = v - mu
    var = jnp.mean(c * c, axis=-1, keepdims=True)
    o_ref[...] = c * lax.rsqrt(var + LN_EPS) * g_ref[...] + b_ref[...]


def _merge(ya, yb, yc, proj, x2d, w_branch_bf16, w_out_bf16, ln_g, ln_b, alpha, tm):
    n = x2d.shape[0]
    row = pl.BlockSpec((tm, D_MODEL), lambda i: (i, 0))
    part = lambda k: pl.BlockSpec((None, tm, D_MODEL), lambda i: (k, i, 0))
    vec = pl.BlockSpec((1, D_MODEL), lambda i: (0, 0))
    return pl.pallas_call(
        functools.partial(_merge_kernel, alpha=alpha),
        grid=(n // tm,),
        in_specs=[row, row, row, part(P_GA), part(P_GB), part(P_GC), row,
                  pl.BlockSpec((3, BRANCH_W, D_MODEL), lambda i: (0, 0, 0)),
                  pl.BlockSpec((D_MODEL, D_MODEL), lambda i: (0, 0)),
                  vec, vec],
        out_specs=row,
        out_shape=jax.ShapeDtypeStruct((n, D_MODEL), F32),
        compiler_params=_cparams("parallel"),
    )(ya, yb, yc, proj, proj, proj, x2d, w_branch_bf16, w_out_bf16, ln_g, ln_b)


def _tile(n, pref):
    return pref if n % pref == 0 else n


def _split_w_in(w):
    parts = w.reshape(D_MODEL, N_PROJ, BRANCH_W)
    w_main = parts[:, MAIN_PARTS, :].reshape(D_MODEL, N_MAIN * BRANCH_W).astype(BF16)
    w_kv_t = parts[:, KV_PARTS, :].transpose(1, 2, 0).astype(BF16)
    return w_main, w_kv_t


def _heads_last(kv_t):
    d, b, _, t = kv_t.shape
    return kv_t.reshape(d, b, SB_HEADS, SB_HEAD_DIM, t).transpose(0, 1, 4, 2, 3)


def kernel(x_prompt, x_sample, cache_k, cache_v, state_pool, state_s5_re, state_s5_im, page_table, w_in, pool_w, pool_scale, s5_a_re, s5_a_im, s5_b_re, s5_b_im, s5_c_re, s5_c_im, s5_d, s5_log_dt, w_glu, b_glu, sb_bias, w_branch, w_out, ln_g, ln_b):
    depth = w_in.shape[0]
    alpha = (2.0 * depth) ** 0.25
    bsz, t_len, _ = x_prompt.shape
    dec_b = x_sample.shape[0]
    past_len = page_table.shape[1] * cache_k.shape[2]
    n_prompt = bsz * t_len

    xp = x_prompt.reshape(n_prompt, D_MODEL)
    xs = x_sample.reshape(dec_b, D_MODEL)
    kv_p = kv_s = None
    outs = [[] for _ in range(6)]
    w_main_all, w_kv_t_all = jax.vmap(_split_w_in)(w_in)
    pool_w_all, w_glu_all = pool_w.astype(BF16), w_glu.astype(BF16)
    w_branch_all, w_out_all = w_branch.astype(BF16), w_out.astype(BF16)
    s5_params = (s5_a_re, s5_a_im, s5_b_re, s5_b_im, s5_c_re, s5_c_im, s5_log_dt)
    chunk_ops_all = jax.vmap(_s5_chunk_operators)(*s5_params)
    sample_ops_all = jax.vmap(_s5_sample_operators)(*s5_params)
    for l in range(depth):
        w_main, w_kv_t = w_main_all[l], w_kv_t_all[l]
        pool_w_b, w_glu_b, w_branch_b, w_out_b = pool_w_all[l], w_glu_all[l], w_branch_all[l], w_out_all[l]
        row = lambda a: a[l].reshape(1, -1)
        chunk_ops = tuple(o[l] for o in chunk_ops_all)
        sample_ops = tuple(o[l] for o in sample_ops_all)

        proj, kv_p = _project(xp, w_main, w_kv_t, kv_p, l, depth, bsz, t_len, _tile(t_len, 1024))
        proj4 = proj.reshape(N_MAIN, bsz, t_len, BRANCH_W)
        ya = _pool_prompt(proj4, pool_w_b, row(pool_scale), _tile(t_len, 512)).reshape(n_prompt, BRANCH_W)
        y5, hp_re, hp_im = _s5_prompt(proj, chunk_ops, bsz)
        yb = _glu(y5, proj, row(s5_d), w_glu_b, row(b_glu), _tile(n_prompt, 512))
        yc = _attention_prompt(proj4, kv_p, sb_bias[l], l, _tile(t_len, 2048), 256).reshape(n_prompt, BRANCH_W)
        xp = _merge(ya, yb, yc, proj, xp, w_branch_b, w_out_b, row(ln_g), row(ln_b), alpha, _tile(n_prompt, 256))
        outs[0].append(proj4[P_UA, :, t_len - POOL_BUF:, :])
        outs[1].append(hp_re.reshape(bsz, S5_GROUPS, S5_STATE))
        outs[2].append(hp_im.reshape(bsz, S5_GROUPS, S5_STATE))

        proj_s, kv_s = _project(xs, w_main, w_kv_t, kv_s, l, depth, 1, dec_b, dec_b)
        prev_t = state_pool[l].transpose(1, 0, 2)
        ya_s = _pool_sample(proj_s, prev_t, pool_w_b, row(pool_scale), past_len)
        h_re = state_s5_re[l].reshape(dec_b, S5_GROUPS * S5_STATE)
        h_im = state_s5_im[l].reshape(dec_b, S5_GROUPS * S5_STATE)
        y5_s, n_re, n_im = _s5_sample(proj_s, h_re, h_im, sample_ops)
        yb_s = _glu(y5_s, proj_s, row(s5_d), w_glu_b, row(b_glu), dec_b)
        yc_s = _attention_sample(proj_s, cache_k, cache_v, page_table, sb_bias[l].reshape(SB_HEADS, 1), l)
        xs = _merge(ya_s, yb_s, yc_s, proj_s, xs, w_branch_b, w_out_b, row(ln_g), row(ln_b), alpha, dec_b)
        outs[3].append(jnp.concatenate([prev_t[1:], proj_s[P_UA][None]], axis=0).transpose(1, 0, 2))
        outs[4].append(n_re.reshape(dec_b, S5_GROUPS, S5_STATE))
        outs[5].append(n_im.reshape(dec_b, S5_GROUPS, S5_STATE))

    pool_p, s5r_p, s5i_p, pool_s, s5r_s, s5i_s = (jnp.stack(o) for o in outs)
    k_p, v_p = _heads_last(kv_p[0]), _heads_last(kv_p[1])
    k_s = _heads_last(kv_s[0]).transpose(0, 2, 1, 3, 4)
    v_s = _heads_last(kv_s[1]).transpose(0, 2, 1, 3, 4)
    return (xp.reshape(bsz, t_len, D_MODEL), xs.reshape(dec_b, 1, D_MODEL),
            k_p, v_p, pool_p, s5r_p, s5i_p, k_s, v_s, pool_s, s5r_s, s5i_s)
```

```python
import functools
import math

import jax
import jax.numpy as jnp
from jax import lax
from jax.experimental import pallas as pl
from jax.experimental.pallas import tpu as pltpu

F32 = jnp.float32
BF16 = jnp.bfloat16
HIGHEST = lax.Precision.HIGHEST

D_MODEL = 1024
BRANCH_W = 1024
N_PROJ = 11
MAIN_PARTS = (0, 1, 2, 3, 4, 7, 8, 9, 10)
N_MAIN = len(MAIN_PARTS)
P_UA, P_ZA, P_UB, P_ZB, P_Q, P_ZC, P_GA, P_GB, P_GC = range(N_MAIN)
KV_PARTS = (5, 6)
LOG2E = 1.4426950408889634
POOL_WINDOWS = (2, 4, 8, 16)
POOL_GW = BRANCH_W // len(POOL_WINDOWS)
POOL_BUF = max(POOL_WINDOWS) - 1
POOL_HALO = 16
S5_GROUP = 16
S5_GROUPS = BRANCH_W // S5_GROUP
S5_STATE = 64
S5_CHUNK = 8
S5_SLAB_W = 128
S5_SLAB_G = S5_SLAB_W // S5_GROUP
S5_SLABS = BRANCH_W // S5_SLAB_W
S5_CHUNK_W = S5_CHUNK * S5_SLAB_W
SB_HEADS = 16
SB_HEAD_DIM = BRANCH_W // SB_HEADS
SB_SCALE = 1.0 / math.sqrt(SB_HEAD_DIM)
HEAD_PAIRS = SB_HEADS // 2
PAIR_LANES = 2 * SB_HEAD_DIM
LN_EPS = 1e-5
VMEM_LIMIT = 56 * 1024 * 1024


def _cparams(*sem):
    return pltpu.CompilerParams(dimension_semantics=sem, vmem_limit_bytes=VMEM_LIMIT)


def _silu(z):
    return z * jax.nn.sigmoid(z)


def _softplus(z):
    return jnp.maximum(z, 0.0) + jnp.log(1.0 + jnp.exp2(jnp.abs(z) * (-LOG2E)))


def _proj_kernel(*refs):
    x_ref, wm_ref, wt_ref = refs[:3]
    o_ref, kt_ref, vt_ref, xb_ref = refs[-4:]
    j = pl.program_id(1)

    @pl.when(j == 0)
    def _():
        xb_ref[...] = x_ref[...].astype(BF16)

    @pl.when(j < N_MAIN)
    def _():
        o_ref[...] = jnp.dot(xb_ref[...], wm_ref[...], preferred_element_type=F32)

    for part, t_ref in enumerate((kt_ref, vt_ref)):
        @pl.when(j == N_MAIN + part)
        def _():
            t_ref[...] = lax.dot_general(wt_ref[...], xb_ref[...], (((1,), (1,)), ((), ())),
                                         preferred_element_type=F32)


def _project(x2d, w_all, w_kv_t, kv_bufs, layer, bsz, t_len, tm):
    n = x2d.shape[0]
    tpb = t_len // tm
    n_kv = len(KV_PARTS)
    main_j = lambda j: jnp.minimum(j, N_MAIN - 1)
    kv_j = lambda j: jnp.maximum(j - N_MAIN, 0)
    main_col = lambda j: jnp.where(main_j(j) < KV_PARTS[0], main_j(j), main_j(j) + n_kv)
    kv_spec = pl.BlockSpec((None, None, BRANCH_W, tm), lambda i, j: (layer, i // tpb, 0, i % tpb))
    kv_shape = jax.ShapeDtypeStruct(kv_bufs[0].shape, F32)
    main, k_buf, v_buf = pl.pallas_call(
        _proj_kernel,
        grid=(n // tm, N_MAIN + n_kv),
        in_specs=[pl.BlockSpec((tm, D_MODEL), lambda i, j: (i, 0)),
                  pl.BlockSpec((None, D_MODEL, BRANCH_W), lambda i, j: (layer, 0, main_col(j))),
                  pl.BlockSpec((None, None, BRANCH_W, D_MODEL), lambda i, j: (layer, kv_j(j), 0, 0)),
                  pl.BlockSpec(memory_space=pl.ANY), pl.BlockSpec(memory_space=pl.ANY)],
        out_specs=[pl.BlockSpec((None, tm, BRANCH_W), lambda i, j: (main_j(j), i, 0)), kv_spec, kv_spec],
        out_shape=[jax.ShapeDtypeStruct((N_MAIN, n, BRANCH_W), F32), kv_shape, kv_shape],
        scratch_shapes=[pltpu.VMEM((tm, D_MODEL), BF16)],
        input_output_aliases={3: 1, 4: 2},
        compiler_params=_cparams("parallel", "arbitrary"),
    )(x2d, w_all, w_kv_t, *kv_bufs)
    return main, (k_buf, v_buf)


def _pool_mix(acc, cnt, u, z, w_ref, sc_ref, g):
    sl = slice(g * POOL_GW, (g + 1) * POOL_GW)
    pooled = acc / cnt - u
    mixed = jnp.dot(pooled.astype(BF16), w_ref[g], preferred_element_type=F32)
    return mixed * sc_ref[:, sl] * _silu(z)


def _pool_prompt_kernel(u_ref, halo_ref, z_ref, w_ref, sc_ref, o_ref, ext_ref, *, tm):
    i = pl.program_id(1)
    ext_ref[0:POOL_HALO, :] = jnp.where(i == 0, 0.0, halo_ref[...])
    ext_ref[POOL_HALO:POOL_HALO + tm, :] = u_ref[...]
    pos = i * tm + lax.broadcasted_iota(jnp.int32, (tm, 1), 0)
    for g, w in enumerate(POOL_WINDOWS):
        sl = slice(g * POOL_GW, (g + 1) * POOL_GW)
        acc = ext_ref[POOL_HALO:POOL_HALO + tm, sl]
        for k in range(1, w):
            acc = acc + ext_ref[POOL_HALO - k:POOL_HALO - k + tm, sl]
        cnt = jnp.minimum(pos + 1, w).astype(F32)
        o_ref[:, sl] = _pool_mix(acc, cnt, u_ref[:, sl], z_ref[:, sl], w_ref, sc_ref, g)


def _pool_prompt(proj4, pool_w_bf16, pool_scale, tm):
    _, b, t, _ = proj4.shape
    hb = tm // POOL_HALO
    return pl.pallas_call(
        functools.partial(_pool_prompt_kernel, tm=tm),
        grid=(b, t // tm),
        in_specs=[pl.BlockSpec((None, None, tm, BRANCH_W), lambda bb, i: (P_UA, bb, i, 0)),
                  pl.BlockSpec((None, None, POOL_HALO, BRANCH_W),
                               lambda bb, i: (P_UA, bb, jnp.maximum(i * hb - 1, 0), 0)),
                  pl.BlockSpec((None, None, tm, BRANCH_W), lambda bb, i: (P_ZA, bb, i, 0)),
                  pl.BlockSpec((len(POOL_WINDOWS), POOL_GW, POOL_GW), lambda bb, i: (0, 0, 0)),
                  pl.BlockSpec((1, BRANCH_W), lambda bb, i: (0, 0))],
        out_specs=pl.BlockSpec((None, tm, BRANCH_W), lambda bb, i: (bb, i, 0)),
        out_shape=jax.ShapeDtypeStruct((b, t, BRANCH_W), F32),
        scratch_shapes=[pltpu.VMEM((POOL_HALO + tm, BRANCH_W), F32)],
        compiler_params=_cparams("parallel", "parallel"),
    )(proj4, proj4, proj4, pool_w_bf16, pool_scale)


def _pool_sample_kernel(u_ref, prev_ref, z_ref, w_ref, sc_ref, o_ref, *, pos0):
    for g, w in enumerate(POOL_WINDOWS):
        sl = slice(g * POOL_GW, (g + 1) * POOL_GW)
        acc = u_ref[:, sl]
        for k in range(1, w):
            acc = acc + prev_ref[POOL_BUF - k, :, sl]
        cnt = float(min(pos0 + 1, w))
        o_ref[:, sl] = _pool_mix(acc, cnt, u_ref[:, sl], z_ref[:, sl], w_ref, sc_ref, g)


def _pool_sample(proj_s, prev_t, pool_w_bf16, pool_scale, pos0):
    s = proj_s.shape[1]
    return pl.pallas_call(
        functools.partial(_pool_sample_kernel, pos0=pos0),
        grid=(1,),
        in_specs=[pl.BlockSpec((None, s, BRANCH_W), lambda i: (P_UA, 0, 0)),
                  pl.BlockSpec((POOL_BUF, s, BRANCH_W), lambda i: (0, 0, 0)),
                  pl.BlockSpec((None, s, BRANCH_W), lambda i: (P_ZA, 0, 0)),
                  pl.BlockSpec((len(POOL_WINDOWS), POOL_GW, POOL_GW), lambda i: (0, 0, 0)),
                  pl.BlockSpec((1, BRANCH_W), lambda i: (0, 0))],
        out_specs=pl.BlockSpec((s, BRANCH_W), lambda i: (0, 0)),
        out_shape=jax.ShapeDtypeStruct((s, BRANCH_W), F32),
        compiler_params=_cparams("arbitrary"),
    )(proj_s, prev_t, proj_s, pool_w_bf16, pool_scale)


def _s5_discretize(a_re, a_im, b_re, b_im, c_re, c_im, log_dt):
    lam = lax.complex(a_re, a_im)
    step = lam * jnp.exp(log_dt)[:, None]
    a_bar = jnp.exp(step)
    b_bar = ((a_bar - 1.0) / lam)[..., None] * lax.complex(b_re, b_im)
    c_mat = lax.complex(c_re, c_im)
    return step, a_bar, b_bar, c_mat


def _same_group(rows, row_unit, cols, col_unit, n_groups):
    r = lax.broadcasted_iota(jnp.int32, (rows, cols), 0) // row_unit % n_groups
    c = lax.broadcasted_iota(jnp.int32, (rows, cols), 1) // col_unit % n_groups
    return r == c


def _spread_tokens(m):
    k, n = S5_CHUNK * S5_GROUP, S5_CHUNK_W
    r = lax.broadcasted_iota(jnp.int32, (k, n), 0)
    c = lax.broadcasted_iota(jnp.int32, (k, n), 1)
    sel = ((r // S5_GROUP == c // S5_SLAB_W) & (r % S5_GROUP == c % S5_GROUP)).astype(F32)
    return jnp.einsum('...rk,kc->...rc', m, sel, precision=HIGHEST)


def _s5_chunk_operators(a_re, a_im, b_re, b_im, c_re, c_im, log_dt):
    ell = S5_CHUNK
    step, _, b_bar, c_mat = _s5_discretize(a_re, a_im, b_re, b_im, c_re, c_im, log_dt)
    apow = jnp.exp(step[None] * jnp.arange(ell + 1, dtype=F32)[:, None, None])
    ca = c_mat[None] * apow[:, :, None, :]
    ca_re, ca_im = jnp.real(ca), jnp.imag(ca)
    bb_re, bb_im = jnp.real(b_bar), jnp.imag(b_bar)
    kmat = (jnp.einsum('jgop,gpi->gjoi', ca_re[:ell], bb_re, precision=HIGHEST)
            - jnp.einsum('jgop,gpi->gjoi', ca_im[:ell], bb_im, precision=HIGHEST))
    s_idx = jnp.arange(ell)[:, None]
    t_idx = jnp.arange(ell)[None, :]
    lag = jnp.clip(t_idx - s_idx, 0, ell - 1)
    tfull = jnp.where((t_idx >= s_idx)[None, :, :, None, None], kmat[:, lag], 0.0)
    ns, ng, sw = S5_SLABS, S5_SLAB_G, S5_SLAB_G * S5_STATE
    t_c = tfull.reshape(ns, ng, ell, ell, S5_GROUP, S5_GROUP).transpose(0, 2, 1, 5, 3, 4)
    t_c = t_c.reshape(ns, S5_CHUNK_W, ell * S5_GROUP)
    t_op = jnp.where(_same_group(S5_CHUNK_W, S5_GROUP, S5_CHUNK_W, S5_GROUP, ng), _spread_tokens(t_c), 0.0)
    w_c = apow[ell - 1 - jnp.arange(ell)][:, :, :, None] * b_bar[None]
    w_c = w_c.reshape(ell, ns, ng, S5_STATE, S5_GROUP).transpose(1, 0, 2, 4, 3)
    w_c = w_c.reshape(ns, S5_CHUNK_W, S5_STATE)
    w_mask = _same_group(S5_CHUNK_W, S5_GROUP, sw, S5_STATE, ng)
    w_half = lambda m: jnp.where(w_mask, jnp.tile(m, (1, 1, ng)), 0.0)
    w_op = jnp.concatenate([w_half(jnp.real(w_c)), w_half(jnp.imag(w_c))], axis=2)
    v_mask = _same_group(sw, S5_STATE, S5_CHUNK_W, S5_GROUP, ng)

    def v_half(m):
        m = m.transpose(1, 3, 0, 2).reshape(ns, sw, ell * S5_GROUP)
        return jnp.where(v_mask, _spread_tokens(m), 0.0)

    v_op = jnp.concatenate([v_half(ca_re[1:]), v_half(-ca_im[1:])], axis=1)
    a_l = apow[ell]
    a_r = jnp.real(a_l).reshape(1, S5_GROUPS * S5_STATE)
    a_i = jnp.imag(a_l).reshape(1, S5_GROUPS * S5_STATE)
    return t_op.astype(BF16), w_op.astype(BF16), v_op.astype(BF16), a_r, a_i


def _s5_gather_chunks(u_ref, tr):
    rows = [u_ref[pl.ds(s, tr, stride=S5_CHUNK), :] for s in range(S5_CHUNK)]
    return jnp.concatenate(rows, axis=1).astype(BF16)


def _s5_state_kernel(u_ref, w_ref, o_ref, *, tr):
    o_ref[...] = jnp.dot(_s5_gather_chunks(u_ref, tr), w_ref[...], preferred_element_type=F32)


def _s5_chunk_states(proj, w_op, tr):
    n = proj.shape[1]
    r = n // S5_CHUNK
    return pl.pallas_call(
        functools.partial(_s5_state_kernel, tr=tr),
        grid=(S5_SLABS, r // tr),
        in_specs=[pl.BlockSpec((None, tr * S5_CHUNK, S5_SLAB_W), lambda g, i: (P_UB, i, g)),
                  pl.BlockSpec((None, S5_CHUNK_W, S5_CHUNK_W), lambda g, i: (g, 0, 0))],
        out_specs=pl.BlockSpec((tr, S5_CHUNK_W), lambda g, i: (i, g)),
        out_shape=jax.ShapeDtypeStruct((r, S5_SLABS * S5_CHUNK_W), F32),
        compiler_params=_cparams("parallel", "parallel"),
    )(proj, w_op)


def _s5_scan_kernel(s_ref, ar_ref, ai_ref, hprev_ref, hr_ref, hi_ref, hr_scr, hi_scr, *, cb):
    j = pl.program_id(1)
    sw = S5_SLAB_G * S5_STATE

    @pl.when(j == 0)
    def _():
        hr_scr[...] = jnp.zeros_like(hr_scr)
        hi_scr[...] = jnp.zeros_like(hi_scr)

    for n in range(S5_SLABS):
        re = slice(n * 2 * sw, n * 2 * sw + sw)
        im = slice(n * 2 * sw + sw, (n + 1) * 2 * sw)
        st = slice(n * sw, (n + 1) * sw)
        a_r = ar_ref[:, st]
        a_i = ai_ref[:, st]

        def body(c, carry, re=re, im=im, a_r=a_r, a_i=a_i):
            h_r, h_i = carry
            row = pl.ds(c, 1)
            hprev_ref[row, re] = h_r
            hprev_ref[row, im] = h_i
            n_r = a_r * h_r - a_i * h_i + s_ref[row, re]
            n_i = a_r * h_i + a_i * h_r + s_ref[row, im]
            return n_r, n_i

        h_r, h_i = lax.fori_loop(0, cb, body, (hr_scr[:, st], hi_scr[:, st]))
        hr_scr[:, st] = h_r
        hi_scr[:, st] = h_i

    @pl.when(j == pl.num_programs(1) - 1)
    def _():
        hr_ref[...] = hr_scr[...]
        hi_ref[...] = hi_scr[...]


def _s5_scan(s_loc, a_r, a_i, bsz, cb):
    r, w = s_loc.shape
    nj = r // bsz // cb
    sw = a_r.shape[1]
    vec = pl.BlockSpec((1, sw), lambda bb, j: (0, 0))
    last = pl.BlockSpec((None, 1, sw), lambda bb, j: (bb, 0, 0))
    blk = pl.BlockSpec((cb, w), lambda bb, j: (bb * nj + j, 0))
    return pl.pallas_call(
        functools.partial(_s5_scan_kernel, cb=cb),
        grid=(bsz, nj),
        in_specs=[blk, vec, vec],
        out_specs=[blk, last, last],
        out_shape=[jax.ShapeDtypeStruct((r, w), F32),
                   jax.ShapeDtypeStruct((bsz, 1, sw), F32),
                   jax.ShapeDtypeStruct((bsz, 1, sw), F32)],
        scratch_shapes=[pltpu.VMEM((1, sw), F32), pltpu.VMEM((1, sw), F32)],
        compiler_params=_cparams("parallel", "arbitrary"),
    )(s_loc, a_r, a_i)


def _s5_out_kernel(u_ref, h_ref, t_ref, v_ref, y_ref, *, tr):
    y = (jnp.dot(_s5_gather_chunks(u_ref, tr), t_ref[...], preferred_element_type=F32)
         + jnp.dot(h_ref[...].astype(BF16), v_ref[...], preferred_element_type=F32))
    for s in range(S5_CHUNK):
        y_ref[pl.ds(s, tr, stride=S5_CHUNK), :] = y[:, s * S5_SLAB_W:(s + 1) * S5_SLAB_W]


def _s5_chunk_outputs(proj, h_prev, t_op, v_op, tr):
    n = proj.shape[1]
    r = n // S5_CHUNK
    op_spec = pl.BlockSpec((None, S5_CHUNK_W, S5_CHUNK_W), lambda g, i: (g, 0, 0))
    return pl.pallas_call(
        functools.partial(_s5_out_kernel, tr=tr),
        grid=(S5_SLABS, r // tr),
        in_specs=[pl.BlockSpec((None, tr * S5_CHUNK, S5_SLAB_W), lambda g, i: (P_UB, i, g)),
                  pl.BlockSpec((tr, S5_CHUNK_W), lambda g, i: (i, g)),
                  op_spec, op_spec],
        out_specs=pl.BlockSpec((tr * S5_CHUNK, S5_SLAB_W), lambda g, i: (i, g)),
        out_shape=jax.ShapeDtypeStruct((n, BRANCH_W), F32),
        compiler_params=_cparams("parallel", "parallel"),
    )(proj, h_prev, t_op, v_op)


def _s5_prompt(proj, ops, bsz):
    t_op, w_op, v_op, a_r, a_i = ops
    r = proj.shape[1] // S5_CHUNK
    tr = _tile(r, 256)
    s_loc = _s5_chunk_states(proj, w_op, tr)
    h_prev, h_re, h_im = _s5_scan(s_loc, a_r, a_i, bsz, _tile(r // bsz, 128))
    return _s5_chunk_outputs(proj, h_prev, t_op, v_op, tr), h_re, h_im


def _s5_sample_operators(a_re, a_im, b_re, b_im, c_re, c_im, log_dt):
    _, a_bar, b_bar, c_mat = _s5_discretize(a_re, a_im, b_re, b_im, c_re, c_im, log_dt)
    ns = BRANCH_W // 256
    gs = S5_GROUPS // ns

    def expand_b(m):
        m = m.reshape(ns, gs, S5_STATE, S5_GROUP).transpose(0, 1, 3, 2)
        m = jnp.tile(m.reshape(ns, gs * S5_GROUP, S5_STATE), (1, 1, gs))
        return jnp.where(_same_group(m.shape[1], S5_GROUP, m.shape[2], S5_STATE, gs), m, 0.0)

    def expand_c(m):
        m = m.reshape(ns, gs, S5_GROUP, S5_STATE).transpose(0, 1, 3, 2)
        m = jnp.tile(m.reshape(ns, gs * S5_STATE, S5_GROUP), (1, 1, gs))
        return jnp.where(_same_group(m.shape[1], S5_STATE, m.shape[2], S5_GROUP, gs), m, 0.0)

    b_op = jnp.concatenate([expand_b(jnp.real(b_bar)), expand_b(jnp.imag(b_bar))], axis=2)
    c_op = jnp.concatenate([expand_c(jnp.real(c_mat)), expand_c(-jnp.imag(c_mat))], axis=1)
    a_r = jnp.real(a_bar).reshape(1, S5_GROUPS * S5_STATE)
    a_i = jnp.imag(a_bar).reshape(1, S5_GROUPS * S5_STATE)
    return b_op.astype(BF16), c_op.astype(BF16), a_r, a_i


def _s5_sample_kernel(u_ref, hr_ref, hi_ref, ar_ref, ai_ref, b_ref, c_ref, y_ref, or_ref, oi_ref):
    ns = b_ref.shape[0]
    cw = b_ref.shape[1]
    sw = c_ref.shape[1] // 2
    for n in range(ns):
        cs = slice(n * cw, (n + 1) * cw)
        ss = slice(n * sw, (n + 1) * sw)
        bu = jnp.dot(u_ref[:, cs].astype(BF16), b_ref[n], preferred_element_type=F32)
        a_r, a_i = ar_ref[:, ss], ai_ref[:, ss]
        h_r, h_i = hr_ref[:, ss], hi_ref[:, ss]
        n_r = a_r * h_r - a_i * h_i + bu[:, :sw]
        n_i = a_r * h_i + a_i * h_r + bu[:, sw:]
        or_ref[:, ss] = n_r
        oi_ref[:, ss] = n_i
        hcat = jnp.concatenate([n_r, n_i], axis=1).astype(BF16)
        y_ref[:, cs] = jnp.dot(hcat, c_ref[n], preferred_element_type=F32)


def _s5_sample(proj_s, h_re, h_im, ops):
    b_op, c_op, a_r, a_i = ops
    s = proj_s.shape[1]
    sw = h_re.shape[1]
    full = lambda shape: pl.BlockSpec(shape, lambda i: (0,) * len(shape))
    return pl.pallas_call(
        _s5_sample_kernel,
        grid=(1,),
        in_specs=[pl.BlockSpec((None, s, BRANCH_W), lambda i: (P_UB, 0, 0)),
                  full((s, sw)), full((s, sw)), full((1, sw)), full((1, sw)),
                  full(b_op.shape), full(c_op.shape)],
        out_specs=[full((s, BRANCH_W)), full((s, sw)), full((s, sw))],
        out_shape=[jax.ShapeDtypeStruct((s, BRANCH_W), F32),
                   jax.ShapeDtypeStruct((s, sw), F32),
                   jax.ShapeDtypeStruct((s, sw), F32)],
        compiler_params=_cparams("arbitrary"),
    )(proj_s, h_re, h_im, a_r, a_i, b_op, c_op)


def _glu_kernel(y_ref, u_ref, z_ref, d_ref, w_ref, b_ref, o_ref):
    y = y_ref[...] + d_ref[...] * u_ref[...]
    g = jax.nn.gelu(y, approximate=True)
    t = jnp.dot(g.astype(BF16), w_ref[...], preferred_element_type=F32) + b_ref[...]
    o_ref[...] = g * jax.nn.sigmoid(t) * _silu(z_ref[...])


def _glu(y2d, proj, s5_d, w_glu_bf16, b_glu, tm):
    n = y2d.shape[0]
    return pl.pallas_call(
        _glu_kernel,
        grid=(n // tm,),
        in_specs=[pl.BlockSpec((tm, BRANCH_W), lambda i: (i, 0)),
                  pl.BlockSpec((None, tm, BRANCH_W), lambda i: (P_UB, i, 0)),
                  pl.BlockSpec((None, tm, BRANCH_W), lambda i: (P_ZB, i, 0)),
                  pl.BlockSpec((1, BRANCH_W), lambda i: (0, 0)),
                  pl.BlockSpec((BRANCH_W, BRANCH_W), lambda i: (0, 0)),
                  pl.BlockSpec((1, BRANCH_W), lambda i: (0, 0))],
        out_specs=pl.BlockSpec((tm, BRANCH_W), lambda i: (i, 0)),
        out_shape=jax.ShapeDtypeStruct((n, BRANCH_W), F32),
        compiler_params=_cparams("parallel"),
    )(y2d, proj, proj, s5_d, w_glu_bf16, b_glu)


def _neg_suffix_ones(n):
    r = lax.broadcasted_iota(jnp.int32, (n, n), 0)
    c = lax.broadcasted_iota(jnp.int32, (n, n), 1)
    return jnp.where(r >= c, -1.0, 0.0).astype(BF16)


def _attn_kernel(bias_ref, q_ref, kt_ref, vt_ref, z_ref, m_ref, o_ref,
                 kb_scr, vb_scr, qm_scr, carry_scr, acc_scr, *, bq, bk):
    hp = pl.program_id(1)
    i = pl.program_id(2)
    lane = lax.broadcasted_iota(jnp.int32, (bq, PAIR_LANES), 1)

    @pl.when(i == 0)
    def _():
        kb_scr[...] = kt_ref[...].astype(BF16)
        vb_scr[...] = vt_ref[...].astype(BF16)

    q = q_ref[...] * SB_SCALE
    qm_scr[0] = jnp.where(lane < SB_HEAD_DIM, q, 0.0).astype(BF16)
    qm_scr[1] = jnp.where(lane >= SB_HEAD_DIM, q, 0.0).astype(BF16)
    carry_scr[...] = jnp.zeros_like(carry_scr)
    acc_scr[...] = jnp.zeros_like(acc_scr)

    def tile(j, r0, rows, masked):
        rs = slice(r0, r0 + rows)
        ks = pl.multiple_of(j * bk, bk)
        kb = kb_scr[:, pl.ds(ks, bk)]
        vb = vb_scr[:, pl.ds(ks, bk)]
        if masked:
            qpos = i * bq + r0 + lax.broadcasted_iota(jnp.int32, (rows, bk), 0)
            kpos = j * bk + lax.broadcasted_iota(jnp.int32, (rows, bk), 1)
            causal = kpos < qpos
        for h in range(2):
            z = jnp.dot(qm_scr[h, rs, :], kb, preferred_element_type=F32)
            z = z + bias_ref[2 * hp + h]
            sp = _softplus(z)
            if masked:
                sp = jnp.where(causal, sp, 0.0)
            nsuf = jnp.dot(sp.astype(BF16), m_ref[...], preferred_element_type=F32)
            carry = carry_scr[h, rs, :]
            expo = (z + nsuf) + jnp.concatenate([carry] * (bk // PAIR_LANES), axis=1)
            w = jnp.exp2(expo * LOG2E)
            if masked:
                w = jnp.where(causal, w, 0.0)
            acc_scr[h, rs, :] += lax.dot_general(w.astype(BF16), vb, (((1,), (1,)), ((), ())),
                                                 preferred_element_type=F32)
            carry_scr[h, rs, :] = carry + jnp.broadcast_to(nsuf[:, 0:1], (rows, PAIR_LANES))

    r = bq // bk
    for jl in reversed(range(r)):
        tile(r * i + jl, jl * bk, bq - jl * bk, True)

    def body(t, c):
        tile(r * i - 1 - t, 0, bq, False)
        return c

    lax.fori_loop(0, r * i, body, 0)

    y = jnp.where(lane < SB_HEAD_DIM, acc_scr[0], acc_scr[1])
    o_ref[...] = y * _silu(z_ref[...])


def _attention_prompt(proj4, kv_bufs, sb_bias, layer, bq, bk):
    _, b, t, _ = proj4.shape
    kv_spec = pl.BlockSpec((None, None, PAIR_LANES, t), lambda bb, hp, i, s: (layer, bb, hp, 0))
    grid_spec = pltpu.PrefetchScalarGridSpec(
        num_scalar_prefetch=1,
        grid=(b, HEAD_PAIRS, t // bq),
        in_specs=[pl.BlockSpec((None, None, bq, PAIR_LANES), lambda bb, hp, i, s: (P_Q, bb, i, hp)),
                  kv_spec, kv_spec,
                  pl.BlockSpec((None, None, bq, PAIR_LANES), lambda bb, hp, i, s: (P_ZC, bb, i, hp)),
                  pl.BlockSpec((bk, bk), lambda bb, hp, i, s: (0, 0))],
        out_specs=pl.BlockSpec((None, bq, PAIR_LANES), lambda bb, hp, i, s: (bb, i, hp)),
        scratch_shapes=[pltpu.VMEM((PAIR_LANES, t), BF16), pltpu.VMEM((PAIR_LANES, t), BF16),
                        pltpu.VMEM((2, bq, PAIR_LANES), BF16),
                        pltpu.VMEM((2, bq, PAIR_LANES), F32), pltpu.VMEM((2, bq, PAIR_LANES), F32)])
    return pl.pallas_call(
        functools.partial(_attn_kernel, bq=bq, bk=bk),
        grid_spec=grid_spec,
        out_shape=jax.ShapeDtypeStruct((b, t, BRANCH_W), F32),
        compiler_params=_cparams("parallel", "parallel", "arbitrary"),
    )(sb_bias, proj4, kv_bufs[0], kv_bufs[1], proj4, _neg_suffix_ones(bk))


def _decode_kernel(pt_ref, qt_ref, z_ref, *refs, pps):
    k_refs = refs[:pps]
    v_refs = refs[pps:2 * pps]
    bias_ref, m_ref, o_ref, qb_scr, carry_scr, acc_scr = refs[2 * pps:]
    s = pl.program_id(1)
    page = k_refs[0].shape[2]

    @pl.when(s == 0)
    def _():
        qt = qt_ref[...] * SB_SCALE
        for h in range(SB_HEADS):
            qb_scr[h] = jnp.broadcast_to(qt[:, h:h + 1], (SB_HEAD_DIM, page))
        carry_scr[...] = jnp.zeros_like(carry_scr)
        acc_scr[...] = jnp.zeros_like(acc_scr)

    z = jnp.concatenate([jnp.sum(k_refs[p][h] * qb_scr[h], axis=0, keepdims=True)
                         for p in range(pps) for h in range(SB_HEADS)], axis=0) + bias_ref[...]
    sp = _softplus(z)
    hi = sp.astype(BF16)
    lo = (sp - hi.astype(F32)).astype(BF16)
    nsuf = (jnp.dot(hi, m_ref[...], preferred_element_type=F32)
            + jnp.dot(lo, m_ref[...], preferred_element_type=F32))
    carry = carry_scr[...]
    ws = []
    for p in range(pps):
        rows = slice(p * SB_HEADS, (p + 1) * SB_HEADS)
        ws.append(jnp.exp2(((z[rows] + nsuf[rows]) + carry) * LOG2E))
        carry = carry + jnp.broadcast_to(nsuf[rows][:, 0:1], (SB_HEADS, page))
    carry_scr[...] = carry
    for h in range(SB_HEADS):
        acc = acc_scr[h]
        for p in range(pps):
            acc = acc + v_refs[p][h] * jnp.broadcast_to(ws[p][h:h + 1, :], (SB_HEAD_DIM, page))
        acc_scr[h] = acc

    @pl.when(s == pl.num_programs(1) - 1)
    def _():
        y = jnp.sum(acc_scr[...], axis=-1)
        o_ref[...] = y * _silu(z_ref[...])


def _attention_sample(proj_s, cache_k, cache_v, page_table, bias_col, layer):
    s = proj_s.shape[1]
    n_pages = page_table.shape[1]
    page = cache_k.shape[2]
    pps = next(c for c in (8, 4, 2, 1) if n_pages % c == 0)
    ck = cache_k.transpose(0, 1, 3, 4, 2)
    cv = cache_v.transpose(0, 1, 3, 4, 2)
    heads = lambda part: proj_s[part].reshape(s, SB_HEADS, SB_HEAD_DIM)
    q_t = heads(P_Q).transpose(0, 2, 1)

    def page_spec(j):
        return pl.BlockSpec((None, None, SB_HEADS, SB_HEAD_DIM, page),
                            lambda bb, st, pt: (layer, pt[bb, n_pages - 1 - (st * pps + j)], 0, 0, 0))

    grid_spec = pltpu.PrefetchScalarGridSpec(
        num_scalar_prefetch=1,
        grid=(s, n_pages // pps),
        in_specs=[pl.BlockSpec((None, SB_HEAD_DIM, SB_HEADS), lambda bb, st, pt: (bb, 0, 0)),
                  pl.BlockSpec((None, SB_HEADS, SB_HEAD_DIM), lambda bb, st, pt: (bb, 0, 0))]
                 + [page_spec(j) for j in range(pps)] * 2
                 + [pl.BlockSpec((pps * SB_HEADS, 1), lambda bb, st, pt: (0, 0)),
                    pl.BlockSpec((page, page), lambda bb, st, pt: (0, 0))],
        out_specs=pl.BlockSpec((None, SB_HEADS, SB_HEAD_DIM), lambda bb, st, pt: (bb, 0, 0)),
        scratch_shapes=[pltpu.VMEM((SB_HEADS, SB_HEAD_DIM, page), F32),
                        pltpu.VMEM((SB_HEADS, page), F32),
                        pltpu.VMEM((SB_HEADS, SB_HEAD_DIM, page), F32)])
    out = pl.pallas_call(
        functools.partial(_decode_kernel, pps=pps),
        grid_spec=grid_spec,
        out_shape=jax.ShapeDtypeStruct((s, SB_HEADS, SB_HEAD_DIM), F32),
        compiler_params=_cparams("parallel", "arbitrary"),
    )(page_table, q_t, heads(P_ZC), *([ck] * pps), *([cv] * pps), jnp.tile(bias_col, (pps, 1)), _neg_suffix_ones(page))
    return out.reshape(s, BRANCH_W)


def _merge_kernel(ya_ref, yb_ref, yc_ref, ga_ref, gb_ref, gc_ref, x_ref, wb_ref, wo_ref, g_ref, b_ref, o_ref,
                  *, alpha):
    merged = None
    for n, (y_ref, gate_ref) in enumerate(((ya_ref, ga_ref), (yb_ref, gb_ref), (yc_ref, gc_ref))):
        pb = jnp.dot(y_ref[...].astype(BF16), wb_ref[n], preferred_element_type=F32)
        term = jax.nn.sigmoid(gate_ref[...]) * pb
        merged = term if merged is None else merged + term
    out = jnp.dot(merged.astype(BF16), wo_ref[...], preferred_element_type=F32)
    v = alpha * x_ref[...] + out
    mu = jnp.mean(v, axis=-1, keepdims=True)
    c = v - mu
    var = jnp.mean(c * c, axis=-1, keepdims=True)
    o_ref[...] = c * lax.rsqrt(var + LN_EPS) * g_ref[...] + b_ref[...]


def _merge(ya, yb, yc, proj, x2d, w_branch_bf16, w_out_bf16, ln_g, ln_b, alpha, tm):
    n = x2d.shape[0]
    row = pl.BlockSpec((tm, D_MODEL), lambda i: (i, 0))
    part = lambda k: pl.BlockSpec((None, tm, D_MODEL), lambda i: (k, i, 0))
    vec = pl.BlockSpec((1, D_MODEL), lambda i: (0, 0))
    return pl.pallas_call(
        functools.partial(_merge_kernel, alpha=alpha),
        grid=(n // tm,),
        in_specs=[row, row, row, part(P_GA), part(P_GB), part(P_GC), row,
                  pl.BlockSpec((3, BRANCH_W, D_MODEL), lambda i: (0, 0, 0)),
                  pl.BlockSpec((D_MODEL, D_MODEL), lambda i: (0, 0)),
                  vec, vec],
        out_specs=row,
        out_shape=jax.ShapeDtypeStruct((n, D_MODEL), F32),
        compiler_params=_cparams("parallel"),
    )(ya, yb, yc, proj, proj, proj, x2d, w_branch_bf16, w_out_bf16, ln_g, ln_b)


def _tile(n, pref):
    return pref if n % pref == 0 else n


def _kv_weights_t(w_in):
    lo, hi = KV_PARTS[0] * BRANCH_W, (KV_PARTS[-1] + 1) * BRANCH_W
    w_kv = w_in[:, :, lo:hi].reshape(w_in.shape[0], D_MODEL, len(KV_PARTS), BRANCH_W)
    return w_kv.transpose(0, 2, 3, 1).astype(BF16)


def _heads_last(kv_t):
    d, b, _, t = kv_t.shape
    return kv_t.reshape(d, b, SB_HEADS, SB_HEAD_DIM, t).transpose(0, 1, 4, 2, 3)


def kernel(x_prompt, x_sample, cache_k, cache_v, state_pool, state_s5_re, state_s5_im, page_table, w_in, pool_w, pool_scale, s5_a_re, s5_a_im, s5_b_re, s5_b_im, s5_c_re, s5_c_im, s5_d, s5_log_dt, w_glu, b_glu, sb_bias, w_branch, w_out, ln_g, ln_b):
    depth = w_in.shape[0]
    alpha = (2.0 * depth) ** 0.25
    bsz, t_len, _ = x_prompt.shape
    dec_b = x_sample.shape[0]
    past_len = page_table.shape[1] * cache_k.shape[2]
    n_prompt = bsz * t_len

    xp = x_prompt.reshape(n_prompt, D_MODEL)
    xs = x_sample.reshape(dec_b, D_MODEL)
    kv_p = (jnp.zeros((depth, bsz, BRANCH_W, t_len), F32),) * 2
    kv_s = (jnp.zeros((depth, 1, BRANCH_W, dec_b), F32),) * 2
    outs = [[] for _ in range(6)]
    w_in_all, w_kv_t_all = w_in.astype(BF16), _kv_weights_t(w_in)
    pool_w_all, w_glu_all = pool_w.astype(BF16), w_glu.astype(BF16)
    w_branch_all, w_out_all = w_branch.astype(BF16), w_out.astype(BF16)
    s5_params = (s5_a_re, s5_a_im, s5_b_re, s5_b_im, s5_c_re, s5_c_im, s5_log_dt)
    chunk_ops_all = jax.vmap(_s5_chunk_operators)(*s5_params)
    sample_ops_all = jax.vmap(_s5_sample_operators)(*s5_params)
    for l in range(depth):
        pool_w_b, w_glu_b, w_branch_b, w_out_b = pool_w_all[l], w_glu_all[l], w_branch_all[l], w_out_all[l]
        row = lambda a: a[l].reshape(1, -1)
        chunk_ops = tuple(o[l] for o in chunk_ops_all)
        sample_ops = tuple(o[l] for o in sample_ops_all)

        proj, kv_p = _project(xp, w_in_all, w_kv_t_all, kv_p, l, bsz, t_len, _tile(t_len, 1024))
        proj4 = proj.reshape(N_MAIN, bsz, t_len, BRANCH_W)
        ya = _pool_prompt(proj4, pool_w_b, row(pool_scale), _tile(t_len, 512)).reshape(n_prompt, BRANCH_W)
        y5, hp_re, hp_im = _s5_prompt(proj, chunk_ops, bsz)
        yb = _glu(y5, proj, row(s5_d), w_glu_b, row(b_glu), _tile(n_prompt, 512))
        yc = _attention_prompt(proj4, kv_p, sb_bias[l], l, _tile(t_len, 2048), 256).reshape(n_prompt, BRANCH_W)
        xp = _merge(ya, yb, yc, proj, xp, w_branch_b, w_out_b, row(ln_g), row(ln_b), alpha, _tile(n_prompt, 256))
        outs[0].append(proj4[P_UA, :, t_len - POOL_BUF:, :])
        outs[1].append(hp_re.reshape(bsz, S5_GROUPS, S5_STATE))
        outs[2].append(hp_im.reshape(bsz, S5_GROUPS, S5_STATE))

        proj_s, kv_s = _project(xs, w_in_all, w_kv_t_all, kv_s, l, 1, dec_b, dec_b)
        prev_t = state_pool[l].transpose(1, 0, 2)
        ya_s = _pool_sample(proj_s, prev_t, pool_w_b, row(pool_scale), past_len)
        h_re = state_s5_re[l].reshape(dec_b, S5_GROUPS * S5_STATE)
        h_im = state_s5_im[l].reshape(dec_b, S5_GROUPS * S5_STATE)
        y5_s, n_re, n_im = _s5_sample(proj_s, h_re, h_im, sample_ops)
        yb_s = _glu(y5_s, proj_s, row(s5_d), w_glu_b, row(b_glu), dec_b)
        yc_s = _attention_sample(proj_s, cache_k, cache_v, page_table, sb_bias[l].reshape(SB_HEADS, 1), l)
        xs = _merge(ya_s, yb_s, yc_s, proj_s, xs, w_branch_b, w_out_b, row(ln_g), row(ln_b), alpha, dec_b)
        outs[3].append(jnp.concatenate([prev_t[1:], proj_s[P_UA][None]], axis=0).transpose(1, 0, 2))
        outs[4].append(n_re.reshape(dec_b, S5_GROUPS, S5_STATE))
        outs[5].append(n_im.reshape(dec_b, S5_GROUPS, S5_STATE))

    pool_p, s5r_p, s5i_p, pool_s, s5r_s, s5i_s = (jnp.stack(o) for o in outs)
    k_p, v_p = _heads_last(kv_p[0]), _heads_last(kv_p[1])
    k_s = _heads_last(kv_s[0]).transpose(0, 2, 1, 3, 4)
    v_s = _heads_last(kv_s[1]).transpose(0, 2, 1, 3, 4)
    return (xp.reshape(bsz, t_len, D_MODEL), xs.reshape(dec_b, 1, D_MODEL),
            k_p, v_p, pool_p, s5r_p, s5i_p, k_s, v_s, pool_s, s5r_s, s5i_s)
```

```python
import functools
import math

import jax
import jax.numpy as jnp
from jax import lax
from jax.experimental import pallas as pl
from jax.experimental.pallas import tpu as pltpu

F32 = jnp.float32
BF16 = jnp.bfloat16
HIGHEST = lax.Precision.HIGHEST

D_MODEL = 1024
BRANCH_W = 1024
MAIN_PARTS = (0, 1, 2, 3, 4, 7, 8, 9, 10)
N_MAIN = len(MAIN_PARTS)
P_UA, P_ZA, P_UB, P_ZB, P_Q, P_ZC, P_GA, P_GB, P_GC = range(N_MAIN)
KV_PARTS = (5, 6)
LOG2E = 1.4426950408889634
POOL_WINDOWS = (2, 4, 8, 16)
POOL_GW = BRANCH_W // len(POOL_WINDOWS)
POOL_BUF = max(POOL_WINDOWS) - 1
POOL_HALO = 16
S5_GROUP = 16
S5_GROUPS = BRANCH_W // S5_GROUP
S5_STATE = 64
S5_CHUNK = 8
S5_SLAB_W = 128
S5_SLAB_G = S5_SLAB_W // S5_GROUP
S5_SLABS = BRANCH_W // S5_SLAB_W
S5_CHUNK_W = S5_CHUNK * S5_SLAB_W
SB_HEADS = 16
SB_HEAD_DIM = BRANCH_W // SB_HEADS
SB_SCALE = 1.0 / math.sqrt(SB_HEAD_DIM)
HEAD_PAIRS = SB_HEADS // 2
PAIR_LANES = 2 * SB_HEAD_DIM
LN_EPS = 1e-5
VMEM_LIMIT = 56 * 1024 * 1024


def _cparams(*sem):
    return pltpu.CompilerParams(dimension_semantics=sem, vmem_limit_bytes=VMEM_LIMIT)


def _silu(z):
    return z * jax.nn.sigmoid(z)


def _softplus(z):
    return jnp.maximum(z, 0.0) + jnp.log(1.0 + jnp.exp2(jnp.abs(z) * (-LOG2E)))


def _proj_kernel(*refs):
    x_ref, wm_ref, wt_ref = refs[:3]
    o_ref, kt_ref, vt_ref, xb_ref = refs[-4:]
    j = pl.program_id(1)

    @pl.when(j == 0)
    def _():
        xb_ref[...] = x_ref[...].astype(BF16)

    @pl.when(j < N_MAIN)
    def _():
        o_ref[...] = jnp.dot(xb_ref[...], wm_ref[...], preferred_element_type=F32)

    for part, t_ref in enumerate((kt_ref, vt_ref)):
        @pl.when(j == N_MAIN + part)
        def _():
            t_ref[...] = lax.dot_general(wt_ref[...], xb_ref[...], (((1,), (1,)), ((), ())),
                                         preferred_element_type=F32)


def _project(x2d, w_all, w_kv_t, kv_bufs, layer, bsz, t_len, tm):
    n = x2d.shape[0]
    tpb = t_len // tm
    n_kv = len(KV_PARTS)
    main_j = lambda j: jnp.minimum(j, N_MAIN - 1)
    kv_j = lambda j: jnp.maximum(j - N_MAIN, 0)
    main_col = lambda j: jnp.where(main_j(j) < KV_PARTS[0], main_j(j), main_j(j) + n_kv)
    kv_spec = pl.BlockSpec((None, None, BRANCH_W, tm), lambda i, j: (layer, i // tpb, 0, i % tpb))
    kv_shape = jax.ShapeDtypeStruct(kv_bufs[0].shape, F32)
    main, k_buf, v_buf = pl.pallas_call(
        _proj_kernel,
        grid=(n // tm, N_MAIN + n_kv),
        in_specs=[pl.BlockSpec((tm, D_MODEL), lambda i, j: (i, 0)),
                  pl.BlockSpec((None, D_MODEL, BRANCH_W), lambda i, j: (layer, 0, main_col(j))),
                  pl.BlockSpec((None, None, BRANCH_W, D_MODEL), lambda i, j: (layer, kv_j(j), 0, 0)),
                  pl.BlockSpec(memory_space=pl.ANY), pl.BlockSpec(memory_space=pl.ANY)],
        out_specs=[pl.BlockSpec((None, tm, BRANCH_W), lambda i, j: (main_j(j), i, 0)), kv_spec, kv_spec],
        out_shape=[jax.ShapeDtypeStruct((N_MAIN, n, BRANCH_W), F32), kv_shape, kv_shape],
        scratch_shapes=[pltpu.VMEM((tm, D_MODEL), BF16)],
        input_output_aliases={3: 1, 4: 2},
        compiler_params=_cparams("parallel", "arbitrary"),
    )(x2d, w_all, w_kv_t, *kv_bufs)
    return main, (k_buf, v_buf)


def _pool_mix(acc, cnt, u, z, w_ref, sc_ref, g):
    sl = slice(g * POOL_GW, (g + 1) * POOL_GW)
    pooled = acc / cnt - u
    mixed = jnp.dot(pooled.astype(BF16), w_ref[g], preferred_element_type=F32)
    return mixed * sc_ref[:, sl] * _silu(z)


def _pool_prompt_kernel(u_ref, halo_ref, z_ref, w_ref, sc_ref, o_ref, ext_ref, *, tm):
    i = pl.program_id(1)
    ext_ref[0:POOL_HALO, :] = jnp.where(i == 0, 0.0, halo_ref[...])
    ext_ref[POOL_HALO:POOL_HALO + tm, :] = u_ref[...]
    pos = i * tm + lax.broadcasted_iota(jnp.int32, (tm, 1), 0)
    for g, w in enumerate(POOL_WINDOWS):
        sl = slice(g * POOL_GW, (g + 1) * POOL_GW)
        acc = ext_ref[POOL_HALO:POOL_HALO + tm, sl]
        for k in range(1, w):
            acc = acc + ext_ref[POOL_HALO - k:POOL_HALO - k + tm, sl]
        cnt = jnp.minimum(pos + 1, w).astype(F32)
        o_ref[:, sl] = _pool_mix(acc, cnt, u_ref[:, sl], z_ref[:, sl], w_ref, sc_ref, g)


def _pool_prompt(proj4, pool_w_bf16, pool_scale, tm):
    _, b, t, _ = proj4.shape
    hb = tm // POOL_HALO
    return pl.pallas_call(
        functools.partial(_pool_prompt_kernel, tm=tm),
        grid=(b, t // tm),
        in_specs=[pl.BlockSpec((None, None, tm, BRANCH_W), lambda bb, i: (P_UA, bb, i, 0)),
                  pl.BlockSpec((None, None, POOL_HALO, BRANCH_W),
                               lambda bb, i: (P_UA, bb, jnp.maximum(i * hb - 1, 0), 0)),
                  pl.BlockSpec((None, None, tm, BRANCH_W), lambda bb, i: (P_ZA, bb, i, 0)),
                  pl.BlockSpec((len(POOL_WINDOWS), POOL_GW, POOL_GW), lambda bb, i: (0, 0, 0)),
                  pl.BlockSpec((1, BRANCH_W), lambda bb, i: (0, 0))],
        out_specs=pl.BlockSpec((None, tm, BRANCH_W), lambda bb, i: (bb, i, 0)),
        out_shape=jax.ShapeDtypeStruct((b, t, BRANCH_W), F32),
        scratch_shapes=[pltpu.VMEM((POOL_HALO + tm, BRANCH_W), F32)],
        compiler_params=_cparams("parallel", "parallel"),
    )(proj4, proj4, proj4, pool_w_bf16, pool_scale)


def _pool_sample_kernel(u_ref, prev_ref, z_ref, w_ref, sc_ref, o_ref, *, pos0):
    for g, w in enumerate(POOL_WINDOWS):
        sl = slice(g * POOL_GW, (g + 1) * POOL_GW)
        acc = u_ref[:, sl]
        for k in range(1, w):
            acc = acc + prev_ref[POOL_BUF - k, :, sl]
        cnt = float(min(pos0 + 1, w))
        o_ref[:, sl] = _pool_mix(acc, cnt, u_ref[:, sl], z_ref[:, sl], w_ref, sc_ref, g)


def _pool_sample(proj_s, prev_t, pool_w_bf16, pool_scale, pos0):
    s = proj_s.shape[1]
    return pl.pallas_call(
        functools.partial(_pool_sample_kernel, pos0=pos0),
        grid=(1,),
        in_specs=[pl.BlockSpec((None, s, BRANCH_W), lambda i: (P_UA, 0, 0)),
                  pl.BlockSpec((POOL_BUF, s, BRANCH_W), lambda i: (0, 0, 0)),
                  pl.BlockSpec((None, s, BRANCH_W), lambda i: (P_ZA, 0, 0)),
                  pl.BlockSpec((len(POOL_WINDOWS), POOL_GW, POOL_GW), lambda i: (0, 0, 0)),
                  pl.BlockSpec((1, BRANCH_W), lambda i: (0, 0))],
        out_specs=pl.BlockSpec((s, BRANCH_W), lambda i: (0, 0)),
        out_shape=jax.ShapeDtypeStruct((s, BRANCH_W), F32),
        compiler_params=_cparams("arbitrary"),
    )(proj_s, prev_t, proj_s, pool_w_bf16, pool_scale)


def _s5_discretize(a_re, a_im, b_re, b_im, c_re, c_im, log_dt):
    lam = lax.complex(a_re, a_im)
    step = lam * jnp.exp(log_dt)[:, None]
    a_bar = jnp.exp(step)
    b_bar = ((a_bar - 1.0) / lam)[..., None] * lax.complex(b_re, b_im)
    c_mat = lax.complex(c_re, c_im)
    return step, a_bar, b_bar, c_mat


def _same_group(rows, row_unit, cols, col_unit, n_groups):
    r = lax.broadcasted_iota(jnp.int32, (rows, cols), 0) // row_unit % n_groups
    c = lax.broadcasted_iota(jnp.int32, (rows, cols), 1) // col_unit % n_groups
    return r == c


def _spread_tokens(m):
    k, n = S5_CHUNK * S5_GROUP, S5_CHUNK_W
    r = lax.broadcasted_iota(jnp.int32, (k, n), 0)
    c = lax.broadcasted_iota(jnp.int32, (k, n), 1)
    sel = ((r // S5_GROUP == c // S5_SLAB_W) & (r % S5_GROUP == c % S5_GROUP)).astype(F32)
    return jnp.einsum('...rk,kc->...rc', m, sel, precision=HIGHEST)


def _s5_chunk_operators(a_re, a_im, b_re, b_im, c_re, c_im, log_dt):
    ell = S5_CHUNK
    step, _, b_bar, c_mat = _s5_discretize(a_re, a_im, b_re, b_im, c_re, c_im, log_dt)
    apow = jnp.exp(step[None] * jnp.arange(ell + 1, dtype=F32)[:, None, None])
    ca = c_mat[None] * apow[:, :, None, :]
    ca_re, ca_im = jnp.real(ca), jnp.imag(ca)
    bb_re, bb_im = jnp.real(b_bar), jnp.imag(b_bar)
    kmat = (jnp.einsum('jgop,gpi->gjoi', ca_re[:ell], bb_re, precision=HIGHEST)
            - jnp.einsum('jgop,gpi->gjoi', ca_im[:ell], bb_im, precision=HIGHEST))
    s_idx = jnp.arange(ell)[:, None]
    t_idx = jnp.arange(ell)[None, :]
    lag = jnp.clip(t_idx - s_idx, 0, ell - 1)
    tfull = jnp.where((t_idx >= s_idx)[None, :, :, None, None], kmat[:, lag], 0.0)
    ns, ng, sw = S5_SLABS, S5_SLAB_G, S5_SLAB_G * S5_STATE
    t_c = tfull.reshape(ns, ng, ell, ell, S5_GROUP, S5_GROUP).transpose(0, 2, 1, 5, 3, 4)
    t_c = t_c.reshape(ns, S5_CHUNK_W, ell * S5_GROUP)
    t_op = jnp.where(_same_group(S5_CHUNK_W, S5_GROUP, S5_CHUNK_W, S5_GROUP, ng), _spread_tokens(t_c), 0.0)
    w_c = apow[ell - 1 - jnp.arange(ell)][:, :, :, None] * b_bar[None]
    w_c = w_c.reshape(ell, ns, ng, S5_STATE, S5_GROUP).transpose(1, 0, 2, 4, 3)
    w_c = w_c.reshape(ns, S5_CHUNK_W, S5_STATE)
    w_mask = _same_group(S5_CHUNK_W, S5_GROUP, sw, S5_STATE, ng)
    w_half = lambda m: jnp.where(w_mask, jnp.tile(m, (1, 1, ng)), 0.0)
    w_op = jnp.concatenate([w_half(jnp.real(w_c)), w_half(jnp.imag(w_c))], axis=2)
    v_mask = _same_group(sw, S5_STATE, S5_CHUNK_W, S5_GROUP, ng)

    def v_half(m):
        m = m.transpose(1, 3, 0, 2).reshape(ns, sw, ell * S5_GROUP)
        return jnp.where(v_mask, _spread_tokens(m), 0.0)

    v_op = jnp.concatenate([v_half(ca_re[1:]), v_half(-ca_im[1:])], axis=1)
    a_l = apow[ell]
    a_r = jnp.real(a_l).reshape(1, S5_GROUPS * S5_STATE)
    a_i = jnp.imag(a_l).reshape(1, S5_GROUPS * S5_STATE)
    return t_op.astype(BF16), w_op.astype(BF16), v_op.astype(BF16), a_r, a_i


def _s5_gather_chunks(u_ref, tr):
    rows = [u_ref[pl.ds(s, tr, stride=S5_CHUNK), :] for s in range(S5_CHUNK)]
    return jnp.concatenate(rows, axis=1).astype(BF16)


def _s5_state_kernel(u_ref, w_ref, o_ref, *, tr):
    o_ref[...] = jnp.dot(_s5_gather_chunks(u_ref, tr), w_ref[...], preferred_element_type=F32)


def _s5_chunk_states(proj, w_op, tr):
    n = proj.shape[1]
    r = n // S5_CHUNK
    return pl.pallas_call(
        functools.partial(_s5_state_kernel, tr=tr),
        grid=(S5_SLABS, r // tr),
        in_specs=[pl.BlockSpec((None, tr * S5_CHUNK, S5_SLAB_W), lambda g, i: (P_UB, i, g)),
                  pl.BlockSpec((None, S5_CHUNK_W, S5_CHUNK_W), lambda g, i: (g, 0, 0))],
        out_specs=pl.BlockSpec((tr, S5_CHUNK_W), lambda g, i: (i, g)),
        out_shape=jax.ShapeDtypeStruct((r, S5_SLABS * S5_CHUNK_W), F32),
        compiler_params=_cparams("parallel", "parallel"),
    )(proj, w_op)


def _s5_scan_kernel(s_ref, ar_ref, ai_ref, hprev_ref, hr_ref, hi_ref, hr_scr, hi_scr, *, cb):
    j = pl.program_id(1)
    sw = S5_SLAB_G * S5_STATE

    @pl.when(j == 0)
    def _():
        hr_scr[...] = jnp.zeros_like(hr_scr)
        hi_scr[...] = jnp.zeros_like(hi_scr)

    for n in range(S5_SLABS):
        re = slice(n * 2 * sw, n * 2 * sw + sw)
        im = slice(n * 2 * sw + sw, (n + 1) * 2 * sw)
        st = slice(n * sw, (n + 1) * sw)
        a_r = ar_ref[:, st]
        a_i = ai_ref[:, st]

        def body(c, carry, re=re, im=im, a_r=a_r, a_i=a_i):
            h_r, h_i = carry
            row = pl.ds(c, 1)
            hprev_ref[row, re] = h_r
            hprev_ref[row, im] = h_i
            n_r = a_r * h_r - a_i * h_i + s_ref[row, re]
            n_i = a_r * h_i + a_i * h_r + s_ref[row, im]
            return n_r, n_i

        h_r, h_i = lax.fori_loop(0, cb, body, (hr_scr[:, st], hi_scr[:, st]))
        hr_scr[:, st] = h_r
        hi_scr[:, st] = h_i

    @pl.when(j == pl.num_programs(1) - 1)
    def _():
        hr_ref[...] = hr_scr[...]
        hi_ref[...] = hi_scr[...]


def _s5_scan(s_loc, a_r, a_i, bsz, cb):
    r, w = s_loc.shape
    nj = r // bsz // cb
    sw = a_r.shape[1]
    vec = pl.BlockSpec((1, sw), lambda bb, j: (0, 0))
    last = pl.BlockSpec((None, 1, sw), lambda bb, j: (bb, 0, 0))
    blk = pl.BlockSpec((cb, w), lambda bb, j: (bb * nj + j, 0))
    return pl.pallas_call(
        functools.partial(_s5_scan_kernel, cb=cb),
        grid=(bsz, nj),
        in_specs=[blk, vec, vec],
        out_specs=[blk, last, last],
        out_shape=[jax.ShapeDtypeStruct((r, w), F32),
                   jax.ShapeDtypeStruct((bsz, 1, sw), F32),
                   jax.ShapeDtypeStruct((bsz, 1, sw), F32)],
        scratch_shapes=[pltpu.VMEM((1, sw), F32), pltpu.VMEM((1, sw), F32)],
        compiler_params=_cparams("parallel", "arbitrary"),
    )(s_loc, a_r, a_i)


def _s5_out_kernel(u_ref, h_ref, t_ref, v_ref, y_ref, *, tr):
    y = (jnp.dot(_s5_gather_chunks(u_ref, tr), t_ref[...], preferred_element_type=F32)
         + jnp.dot(h_ref[...].astype(BF16), v_ref[...], preferred_element_type=F32))
    for s in range(S5_CHUNK):
        y_ref[pl.ds(s, tr, stride=S5_CHUNK), :] = y[:, s * S5_SLAB_W:(s + 1) * S5_SLAB_W]


def _s5_chunk_outputs(proj, h_prev, t_op, v_op, tr):
    n = proj.shape[1]
    r = n // S5_CHUNK
    op_spec = pl.BlockSpec((None, S5_CHUNK_W, S5_CHUNK_W), lambda g, i: (g, 0, 0))
    return pl.pallas_call(
        functools.partial(_s5_out_kernel, tr=tr),
        grid=(S5_SLABS, r // tr),
        in_specs=[pl.BlockSpec((None, tr * S5_CHUNK, S5_SLAB_W), lambda g, i: (P_UB, i, g)),
                  pl.BlockSpec((tr, S5_CHUNK_W), lambda g, i: (i, g)),
                  op_spec, op_spec],
        out_specs=pl.BlockSpec((tr * S5_CHUNK, S5_SLAB_W), lambda g, i: (i, g)),
        out_shape=jax.ShapeDtypeStruct((n, BRANCH_W), F32),
        compiler_params=_cparams("parallel", "parallel"),
    )(proj, h_prev, t_op, v_op)


def _s5_prompt(proj, ops, bsz):
    t_op, w_op, v_op, a_r, a_i = ops
    r = proj.shape[1] // S5_CHUNK
    tr = _tile(r, 256)
    s_loc = _s5_chunk_states(proj, w_op, tr)
    h_prev, h_re, h_im = _s5_scan(s_loc, a_r, a_i, bsz, _tile(r // bsz, 128))
    return _s5_chunk_outputs(proj, h_prev, t_op, v_op, tr), h_re, h_im


def _s5_sample_operators(a_re, a_im, b_re, b_im, c_re, c_im, log_dt):
    _, a_bar, b_bar, c_mat = _s5_discretize(a_re, a_im, b_re, b_im, c_re, c_im, log_dt)
    ns = BRANCH_W // 256
    gs = S5_GROUPS // ns

    def expand_b(m):
        m = m.reshape(ns, gs, S5_STATE, S5_GROUP).transpose(0, 1, 3, 2)
        m = jnp.tile(m.reshape(ns, gs * S5_GROUP, S5_STATE), (1, 1, gs))
        return jnp.where(_same_group(m.shape[1], S5_GROUP, m.shape[2], S5_STATE, gs), m, 0.0)

    def expand_c(m):
        m = m.reshape(ns, gs, S5_GROUP, S5_STATE).transpose(0, 1, 3, 2)
        m = jnp.tile(m.reshape(ns, gs * S5_STATE, S5_GROUP), (1, 1, gs))
        return jnp.where(_same_group(m.shape[1], S5_STATE, m.shape[2], S5_GROUP, gs), m, 0.0)

    b_op = jnp.concatenate([expand_b(jnp.real(b_bar)), expand_b(jnp.imag(b_bar))], axis=2)
    c_op = jnp.concatenate([expand_c(jnp.real(c_mat)), expand_c(-jnp.imag(c_mat))], axis=1)
    a_r = jnp.real(a_bar).reshape(1, S5_GROUPS * S5_STATE)
    a_i = jnp.imag(a_bar).reshape(1, S5_GROUPS * S5_STATE)
    return b_op.astype(BF16), c_op.astype(BF16), a_r, a_i


def _s5_sample_kernel(u_ref, hr_ref, hi_ref, ar_ref, ai_ref, b_ref, c_ref, y_ref, or_ref, oi_ref):
    ns = b_ref.shape[0]
    cw = b_ref.shape[1]
    sw = c_ref.shape[1] // 2
    for n in range(ns):
        cs = slice(n * cw, (n + 1) * cw)
        ss = slice(n * sw, (n + 1) * sw)
        bu = jnp.dot(u_ref[:, cs].astype(BF16), b_ref[n], preferred_element_type=F32)
        a_r, a_i = ar_ref[:, ss], ai_ref[:, ss]
        h_r, h_i = hr_ref[:, ss], hi_ref[:, ss]
        n_r = a_r * h_r - a_i * h_i + bu[:, :sw]
        n_i = a_r * h_i + a_i * h_r + bu[:, sw:]
        or_ref[:, ss] = n_r
        oi_ref[:, ss] = n_i
        hcat = jnp.concatenate([n_r, n_i], axis=1).astype(BF16)
        y_ref[:, cs] = jnp.dot(hcat, c_ref[n], preferred_element_type=F32)


def _s5_sample(proj_s, h_re, h_im, ops):
    b_op, c_op, a_r, a_i = ops
    s = proj_s.shape[1]
    sw = h_re.shape[1]
    full = lambda shape: pl.BlockSpec(shape, lambda i: (0,) * len(shape))
    return pl.pallas_call(
        _s5_sample_kernel,
        grid=(1,),
        in_specs=[pl.BlockSpec((None, s, BRANCH_W), lambda i: (P_UB, 0, 0)),
                  full((s, sw)), full((s, sw)), full((1, sw)), full((1, sw)),
                  full(b_op.shape), full(c_op.shape)],
        out_specs=[full((s, BRANCH_W)), full((s, sw)), full((s, sw))],
        out_shape=[jax.ShapeDtypeStruct((s, BRANCH_W), F32),
                   jax.ShapeDtypeStruct((s, sw), F32),
                   jax.ShapeDtypeStruct((s, sw), F32)],
        compiler_params=_cparams("arbitrary"),
    )(proj_s, h_re, h_im, a_r, a_i, b_op, c_op)


def _glu_kernel(y_ref, u_ref, z_ref, d_ref, w_ref, b_ref, o_ref):
    y = y_ref[...] + d_ref[...] * u_ref[...]
    g = jax.nn.gelu(y, approximate=True)
    t = jnp.dot(g.astype(BF16), w_ref[...], preferred_element_type=F32) + b_ref[...]
    o_ref[...] = g * jax.nn.sigmoid(t) * _silu(z_ref[...])


def _glu(y2d, proj, s5_d, w_glu_bf16, b_glu, tm):
    n = y2d.shape[0]
    return pl.pallas_call(
        _glu_kernel,
        grid=(n // tm,),
        in_specs=[pl.BlockSpec((tm, BRANCH_W), lambda i: (i, 0)),
                  pl.BlockSpec((None, tm, BRANCH_W), lambda i: (P_UB, i, 0)),
                  pl.BlockSpec((None, tm, BRANCH_W), lambda i: (P_ZB, i, 0)),
                  pl.BlockSpec((1, BRANCH_W), lambda i: (0, 0)),
                  pl.BlockSpec((BRANCH_W, BRANCH_W), lambda i: (0, 0)),
                  pl.BlockSpec((1, BRANCH_W), lambda i: (0, 0))],
        out_specs=pl.BlockSpec((tm, BRANCH_W), lambda i: (i, 0)),
        out_shape=jax.ShapeDtypeStruct((n, BRANCH_W), F32),
        compiler_params=_cparams("parallel"),
    )(y2d, proj, proj, s5_d, w_glu_bf16, b_glu)


def _neg_suffix_ones(n):
    r = lax.broadcasted_iota(jnp.int32, (n, n), 0)
    c = lax.broadcasted_iota(jnp.int32, (n, n), 1)
    return jnp.where(r >= c, -1.0, 0.0).astype(BF16)


def _attn_kernel(bias_ref, q_ref, kt_ref, vt_ref, z_ref, m_ref, o_ref,
                 kb_scr, vb_scr, qm_scr, carry_scr, acc_scr, *, bq, bk):
    hp = pl.program_id(1)
    i = pl.program_id(2)
    lane = lax.broadcasted_iota(jnp.int32, (bq, PAIR_LANES), 1)

    @pl.when(i == 0)
    def _():
        kb_scr[...] = kt_ref[...].astype(BF16)
        vb_scr[...] = vt_ref[...].astype(BF16)

    q = q_ref[...] * SB_SCALE
    qm_scr[0] = jnp.where(lane < SB_HEAD_DIM, q, 0.0).astype(BF16)
    qm_scr[1] = jnp.where(lane >= SB_HEAD_DIM, q, 0.0).astype(BF16)
    carry_scr[...] = jnp.zeros_like(carry_scr)
    acc_scr[...] = jnp.zeros_like(acc_scr)

    def tile(j, r0, rows, masked):
        rs = slice(r0, r0 + rows)
        ks = pl.multiple_of(j * bk, bk)
        kb = kb_scr[:, pl.ds(ks, bk)]
        vb = vb_scr[:, pl.ds(ks, bk)]
        if masked:
            qpos = i * bq + r0 + lax.broadcasted_iota(jnp.int32, (rows, bk), 0)
            kpos = j * bk + lax.broadcasted_iota(jnp.int32, (rows, bk), 1)
            causal = kpos < qpos
        for h in range(2):
            z = jnp.dot(qm_scr[h, rs, :], kb, preferred_element_type=F32)
            z = z + bias_ref[2 * hp + h]
            sp = _softplus(z)
            if masked:
                sp = jnp.where(causal, sp, 0.0)
            nsuf = jnp.dot(sp.astype(BF16), m_ref[...], preferred_element_type=F32)
            carry = carry_scr[h, rs, :]
            expo = (z + nsuf) + jnp.concatenate([carry] * (bk // PAIR_LANES), axis=1)
            w = jnp.exp2(expo * LOG2E)
            if masked:
                w = jnp.where(causal, w, 0.0)
            acc_scr[h, rs, :] += lax.dot_general(w.astype(BF16), vb, (((1,), (1,)), ((), ())),
                                                 preferred_element_type=F32)
            carry_scr[h, rs, :] = carry + jnp.broadcast_to(nsuf[:, 0:1], (rows, PAIR_LANES))

    r = bq // bk
    for jl in reversed(range(r)):
        tile(r * i + jl, jl * bk, bq - jl * bk, True)

    def body(t, c):
        tile(r * i - 1 - t, 0, bq, False)
        return c

    lax.fori_loop(0, r * i, body, 0)

    y = jnp.where(lane < SB_HEAD_DIM, acc_scr[0], acc_scr[1])
    o_ref[...] = y * _silu(z_ref[...])


def _attention_prompt(proj4, kv_bufs, sb_bias, layer, bq, bk):
    _, b, t, _ = proj4.shape
    kv_spec = pl.BlockSpec((None, None, PAIR_LANES, t), lambda bb, hp, i, s: (layer, bb, hp, 0))
    grid_spec = pltpu.PrefetchScalarGridSpec(
        num_scalar_prefetch=1,
        grid=(b, HEAD_PAIRS, t // bq),
        in_specs=[pl.BlockSpec((None, None, bq, PAIR_LANES), lambda bb, hp, i, s: (P_Q, bb, i, hp)),
                  kv_spec, kv_spec,
                  pl.BlockSpec((None, None, bq, PAIR_LANES), lambda bb, hp, i, s: (P_ZC, bb, i, hp)),
                  pl.BlockSpec((bk, bk), lambda bb, hp, i, s: (0, 0))],
        out_specs=pl.BlockSpec((None, bq, PAIR_LANES), lambda bb, hp, i, s: (bb, i, hp)),
        scratch_shapes=[pltpu.VMEM((PAIR_LANES, t), BF16), pltpu.VMEM((PAIR_LANES, t), BF16),
                        pltpu.VMEM((2, bq, PAIR_LANES), BF16),
                        pltpu.VMEM((2, bq, PAIR_LANES), F32), pltpu.VMEM((2, bq, PAIR_LANES), F32)])
    return pl.pallas_call(
        functools.partial(_attn_kernel, bq=bq, bk=bk),
        grid_spec=grid_spec,
        out_shape=jax.ShapeDtypeStruct((b, t, BRANCH_W), F32),
        compiler_params=_cparams("parallel", "parallel", "arbitrary"),
    )(sb_bias, proj4, kv_bufs[0], kv_bufs[1], proj4, _neg_suffix_ones(bk))


def _decode_kernel(pt_ref, qt_ref, z_ref, *refs, pps):
    k_refs = refs[:pps]
    v_refs = refs[pps:2 * pps]
    bias_ref, m_ref, o_ref, qb_scr, carry_scr, acc_scr = refs[2 * pps:]
    s = pl.program_id(1)
    page = k_refs[0].shape[2]

    @pl.when(s == 0)
    def _():
        qt = qt_ref[...] * SB_SCALE
        for h in range(SB_HEADS):
            qb_scr[h] = jnp.broadcast_to(qt[:, h:h + 1], (SB_HEAD_DIM, page))
        carry_scr[...] = jnp.zeros_like(carry_scr)
        acc_scr[...] = jnp.zeros_like(acc_scr)

    z = jnp.concatenate([jnp.sum(k_refs[p][h] * qb_scr[h], axis=0, keepdims=True)
                         for p in range(pps) for h in range(SB_HEADS)], axis=0) + bias_ref[...]
    sp = _softplus(z)
    hi = sp.astype(BF16)
    lo = (sp - hi.astype(F32)).astype(BF16)
    nsuf = (jnp.dot(hi, m_ref[...], preferred_element_type=F32)
            + jnp.dot(lo, m_ref[...], preferred_element_type=F32))
    carry = carry_scr[...]
    ws = []
    for p in range(pps):
        rows = slice(p * SB_HEADS, (p + 1) * SB_HEADS)
        ws.append(jnp.exp2(((z[rows] + nsuf[rows]) + carry) * LOG2E))
        carry = carry + jnp.broadcast_to(nsuf[rows][:, 0:1], (SB_HEADS, page))
    carry_scr[...] = carry
    for h in range(SB_HEADS):
        acc = acc_scr[h]
        for p in range(pps):
            acc = acc + v_refs[p][h] * jnp.broadcast_to(ws[p][h:h + 1, :], (SB_HEAD_DIM, page))
        acc_scr[h] = acc

    @pl.when(s == pl.num_programs(1) - 1)
    def _():
        y = jnp.sum(acc_scr[...], axis=-1)
        o_ref[...] = y * _silu(z_ref[...])


def _attention_sample(proj_s, cache_k, cache_v, page_table, bias_col, layer):
    s = proj_s.shape[1]
    n_pages = page_table.shape[1]
    page = cache_k.shape[2]
    pps = next(c for c in (16, 8, 4, 2, 1) if n_pages % c == 0)
    ck = cache_k.transpose(0, 1, 3, 4, 2)
    cv = cache_v.transpose(0, 1, 3, 4, 2)
    heads = lambda part: proj_s[part].reshape(s, SB_HEADS, SB_HEAD_DIM)
    q_t = heads(P_Q).transpose(0, 2, 1)

    def page_spec(j):
        return pl.BlockSpec((None, None, SB_HEADS, SB_HEAD_DIM, page),
                            lambda bb, st, pt: (layer, pt[bb, n_pages - 1 - (st * pps + j)], 0, 0, 0))

    grid_spec = pltpu.PrefetchScalarGridSpec(
        num_scalar_prefetch=1,
        grid=(s, n_pages // pps),
        in_specs=[pl.BlockSpec((None, SB_HEAD_DIM, SB_HEADS), lambda bb, st, pt: (bb, 0, 0)),
                  pl.BlockSpec((None, SB_HEADS, SB_HEAD_DIM), lambda bb, st, pt: (bb, 0, 0))]
                 + [page_spec(j) for j in range(pps)] * 2
                 + [pl.BlockSpec((pps * SB_HEADS, 1), lambda bb, st, pt: (0, 0)),
                    pl.BlockSpec((page, page), lambda bb, st, pt: (0, 0))],
        out_specs=pl.BlockSpec((None, SB_HEADS, SB_HEAD_DIM), lambda bb, st, pt: (bb, 0, 0)),
        scratch_shapes=[pltpu.VMEM((SB_HEADS, SB_HEAD_DIM, page), F32),
                        pltpu.VMEM((SB_HEADS, page), F32),
                        pltpu.VMEM((SB_HEADS, SB_HEAD_DIM, page), F32)])
    out = pl.pallas_call(
        functools.partial(_decode_kernel, pps=pps),
        grid_spec=grid_spec,
        out_shape=jax.ShapeDtypeStruct((s, SB_HEADS, SB_HEAD_DIM), F32),
        compiler_params=_cparams("parallel", "arbitrary"),
    )(page_table, q_t, heads(P_ZC), *([ck] * pps), *([cv] * pps), jnp.tile(bias_col, (pps, 1)), _neg_suffix_ones(page))
    return out.reshape(s, BRANCH_W)


def _merge_kernel(ya_ref, yb_ref, yc_ref, ga_ref, gb_ref, gc_ref, x_ref, wb_ref, wo_ref, g_ref, b_ref, o_ref,
                  *, alpha):
    merged = None
    for n, (y_ref, gate_ref) in enumerate(((ya_ref, ga_ref), (yb_ref, gb_ref), (yc_ref, gc_ref))):
        pb = jnp.dot(y_ref[...].astype(BF16), wb_ref[n], preferred_element_type=F32)
        term = jax.nn.sigmoid(gate_ref[...]) * pb
        merged = term if merged is None else merged + term
    out = jnp.dot(merged.astype(BF16), wo_ref[...], preferred_element_type=F32)
    v = alpha * x_ref[...] + out
    mu = jnp.mean(v, axis=-1, keepdims=True)
    c = v - mu
    var = jnp.mean(c * c, axis=-1, keepdims=True)
    o_ref[...] = c * lax.rsqrt(var + LN_EPS) * g_ref[...] + b_ref[...]


def _merge(ya, yb, yc, proj, x2d, w_branch_bf16, w_out_bf16, ln_g, ln_b, alpha, tm):
    n = x2d.shape[0]
    row = pl.BlockSpec((tm, D_MODEL), lambda i: (i, 0))
    part = lambda k: pl.BlockSpec((None, tm, D_MODEL), lambda i: (k, i, 0))
    vec = pl.BlockSpec((1, D_MODEL), lambda i: (0, 0))
    return pl.pallas_call(
        functools.partial(_merge_kernel, alpha=alpha),
        grid=(n // tm,),
        in_specs=[row, row, row, part(P_GA), part(P_GB), part(P_GC), row,
                  pl.BlockSpec((3, BRANCH_W, D_MODEL), lambda i: (0, 0, 0)),
                  pl.BlockSpec((D_MODEL, D_MODEL), lambda i: (0, 0)),
                  vec, vec],
        out_specs=row,
        out_shape=jax.ShapeDtypeStruct((n, D_MODEL), F32),
        compiler_params=_cparams("parallel"),
    )(ya, yb, yc, proj, proj, proj, x2d, w_branch_bf16, w_out_bf16, ln_g, ln_b)


def _tile(n, pref):
    return pref if n % pref == 0 else n


def _kv_weights_t(w_in):
    lo, hi = KV_PARTS[0] * BRANCH_W, (KV_PARTS[-1] + 1) * BRANCH_W
    w_kv = w_in[:, :, lo:hi].reshape(w_in.shape[0], D_MODEL, len(KV_PARTS), BRANCH_W)
    return w_kv.transpose(0, 2, 3, 1).astype(BF16)


def _heads_last(kv_t):
    d, b, _, t = kv_t.shape
    return kv_t.reshape(d, b, SB_HEADS, SB_HEAD_DIM, t).transpose(0, 1, 4, 2, 3)


def kernel(x_prompt, x_sample, cache_k, cache_v, state_pool, state_s5_re, state_s5_im, page_table, w_in, pool_w, pool_scale, s5_a_re, s5_a_im, s5_b_re, s5_b_im, s5_c_re, s5_c_im, s5_d, s5_log_dt, w_glu, b_glu, sb_bias, w_branch, w_out, ln_g, ln_b):
    depth = w_in.shape[0]
    alpha = (2.0 * depth) ** 0.25
    bsz, t_len, _ = x_prompt.shape
    dec_b = x_sample.shape[0]
    past_len = page_table.shape[1] * cache_k.shape[2]
    n_prompt = bsz * t_len

    xp = x_prompt.reshape(n_prompt, D_MODEL)
    xs = x_sample.reshape(dec_b, D_MODEL)
    kv_p = (jnp.zeros((depth, bsz, BRANCH_W, t_len), F32),) * 2
    kv_s = (jnp.zeros((depth, 1, BRANCH_W, dec_b), F32),) * 2
    outs = [[] for _ in range(6)]
    w_in_all, w_kv_t_all = w_in.astype(BF16), _kv_weights_t(w_in)
    pool_w_all, w_glu_all = pool_w.astype(BF16), w_glu.astype(BF16)
    w_branch_all, w_out_all = w_branch.astype(BF16), w_out.astype(BF16)
    s5_params = (s5_a_re, s5_a_im, s5_b_re, s5_b_im, s5_c_re, s5_c_im, s5_log_dt)
    chunk_ops_all = jax.vmap(_s5_chunk_operators)(*s5_params)
    sample_ops_all = jax.vmap(_s5_sample_operators)(*s5_params)
    for l in range(depth):
        pool_w_b, w_glu_b, w_branch_b, w_out_b = pool_w_all[l], w_glu_all[l], w_branch_all[l], w_out_all[l]
        row = lambda a: a[l].reshape(1, -1)
        chunk_ops = tuple(o[l] for o in chunk_ops_all)
        sample_ops = tuple(o[l] for o in sample_ops_all)

        proj, kv_p = _project(xp, w_in_all, w_kv_t_all, kv_p, l, bsz, t_len, _tile(t_len, 1024))
        proj4 = proj.reshape(N_MAIN, bsz, t_len, BRANCH_W)
        ya = _pool_prompt(proj4, pool_w_b, row(pool_scale), _tile(t_len, 512)).reshape(n_prompt, BRANCH_W)
        y5, hp_re, hp_im = _s5_prompt(proj, chunk_ops, bsz)
        yb = _glu(y5, proj, row(s5_d), w_glu_b, row(b_glu), _tile(n_prompt, 512))
        yc = _attention_prompt(proj4, kv_p, sb_bias[l], l, _tile(t_len, 2048), 256).reshape(n_prompt, BRANCH_W)
        xp = _merge(ya, yb, yc, proj, xp, w_branch_b, w_out_b, row(ln_g), row(ln_b), alpha, _tile(n_prompt, 256))
        outs[0].append(proj4[P_UA, :, t_len - POOL_BUF:, :])
        outs[1].append(hp_re.reshape(bsz, S5_GROUPS, S5_STATE))
        outs[2].append(hp_im.reshape(bsz, S5_GROUPS, S5_STATE))

        proj_s, kv_s = _project(xs, w_in_all, w_kv_t_all, kv_s, l, 1, dec_b, dec_b)
        prev_t = state_pool[l].transpose(1, 0, 2)
        ya_s = _pool_sample(proj_s, prev_t, pool_w_b, row(pool_scale), past_len)
        h_re = state_s5_re[l].reshape(dec_b, S5_GROUPS * S5_STATE)
        h_im = state_s5_im[l].reshape(dec_b, S5_GROUPS * S5_STATE)
        y5_s, n_re, n_im = _s5_sample(proj_s, h_re, h_im, sample_ops)
        yb_s = _glu(y5_s, proj_s, row(s5_d), w_glu_b, row(b_glu), dec_b)
        yc_s = _attention_sample(proj_s, cache_k, cache_v, page_table, sb_bias[l].reshape(SB_HEADS, 1), l)
        xs = _merge(ya_s, yb_s, yc_s, proj_s, xs, w_branch_b, w_out_b, row(ln_g), row(ln_b), alpha, dec_b)
        outs[3].append(jnp.concatenate([prev_t[1:], proj_s[P_UA][None]], axis=0).transpose(1, 0, 2))
        outs[4].append(n_re.reshape(dec_b, S5_GROUPS, S5_STATE))
        outs[5].append(n_im.reshape(dec_b, S5_GROUPS, S5_STATE))

    pool_p, s5r_p, s5i_p, pool_s, s5r_s, s5i_s = (jnp.stack(o) for o in outs)
    k_p, v_p = _heads_last(kv_p[0]), _heads_last(kv_p[1])
    k_s = _heads_last(kv_s[0]).transpose(0, 2, 1, 3, 4)
    v_s = _heads_last(kv_s[1]).transpose(0, 2, 1, 3, 4)
    return (xp.reshape(bsz, t_len, D_MODEL), xs.reshape(dec_b, 1, D_MODEL),
            k_p, v_p, pool_p, s5r_p, s5i_p, k_s, v_s, pool_s, s5r_s, s5i_s)
```

```python
import functools
import math

import jax
import jax.numpy as jnp
from jax import lax
from jax.experimental import pallas as pl
from jax.experimental.pallas import tpu as pltpu

F32 = jnp.float32
BF16 = jnp.bfloat16
HIGHEST = lax.Precision.HIGHEST

D_MODEL = 1024
BRANCH_W = 1024
MAIN_PARTS = (0, 1, 2, 3, 4, 7, 8, 9, 10)
N_MAIN = len(MAIN_PARTS)
P_UA, P_ZA, P_UB, P_ZB, P_Q, P_ZC, P_GA, P_GB, P_GC = range(N_MAIN)
KV_PARTS = (5, 6)
LOG2E = 1.4426950408889634
POOL_WINDOWS = (2, 4, 8, 16)
POOL_GW = BRANCH_W // len(POOL_WINDOWS)
POOL_BUF = max(POOL_WINDOWS) - 1
POOL_HALO = 16
S5_GROUP = 16
S5_GROUPS = BRANCH_W // S5_GROUP
S5_STATE = 64
S5_CHUNK = 8
S5_SLAB_W = 128
S5_SLAB_G = S5_SLAB_W // S5_GROUP
S5_SLABS = BRANCH_W // S5_SLAB_W
S5_CHUNK_W = S5_CHUNK * S5_SLAB_W
SB_HEADS = 16
SB_HEAD_DIM = BRANCH_W // SB_HEADS
SB_SCALE = 1.0 / math.sqrt(SB_HEAD_DIM)
HEAD_PAIRS = SB_HEADS // 2
PAIR_LANES = 2 * SB_HEAD_DIM
LN_EPS = 1e-5
VMEM_LIMIT = 56 * 1024 * 1024


def _cparams(*sem):
    return pltpu.CompilerParams(dimension_semantics=sem, vmem_limit_bytes=VMEM_LIMIT)


def _silu(z):
    return z * jax.nn.sigmoid(z)


def _softplus(z):
    return jnp.maximum(z, 0.0) + jnp.log(1.0 + jnp.exp2(jnp.abs(z) * (-LOG2E)))


def _proj_kernel(*refs):
    x_ref, wm_ref, wt_ref = refs[:3]
    o_ref, kt_ref, vt_ref, xb_ref = refs[-4:]
    j = pl.program_id(1)

    @pl.when(j == 0)
    def _():
        xb_ref[...] = x_ref[...].astype(BF16)

    @pl.when(j < N_MAIN)
    def _():
        o_ref[...] = jnp.dot(xb_ref[...], wm_ref[...], preferred_element_type=F32)

    for part, t_ref in enumerate((kt_ref, vt_ref)):
        @pl.when(j == N_MAIN + part)
        def _():
            t_ref[...] = lax.dot_general(wt_ref[...], xb_ref[...], (((1,), (1,)), ((), ())),
                                         preferred_element_type=F32)


def _project(x2d, w_all, w_kv_t, kv_bufs, layer, bsz, t_len, tm):
    n = x2d.shape[0]
    tpb = t_len // tm
    n_kv = len(KV_PARTS)
    main_j = lambda j: jnp.minimum(j, N_MAIN - 1)
    kv_j = lambda j: jnp.maximum(j - N_MAIN, 0)
    main_col = lambda j: jnp.where(main_j(j) < KV_PARTS[0], main_j(j), main_j(j) + n_kv)
    kv_spec = pl.BlockSpec((None, None, BRANCH_W, tm), lambda i, j: (layer, i // tpb, 0, i % tpb))
    kv_shape = jax.ShapeDtypeStruct(kv_bufs[0].shape, F32)
    main, k_buf, v_buf = pl.pallas_call(
        _proj_kernel,
        grid=(n // tm, N_MAIN + n_kv),
        in_specs=[pl.BlockSpec((tm, D_MODEL), lambda i, j: (i, 0)),
                  pl.BlockSpec((None, D_MODEL, BRANCH_W), lambda i, j: (layer, 0, main_col(j))),
                  pl.BlockSpec((None, None, BRANCH_W, D_MODEL), lambda i, j: (layer, kv_j(j), 0, 0)),
                  pl.BlockSpec(memory_space=pl.ANY), pl.BlockSpec(memory_space=pl.ANY)],
        out_specs=[pl.BlockSpec((None, tm, BRANCH_W), lambda i, j: (main_j(j), i, 0)), kv_spec, kv_spec],
        out_shape=[jax.ShapeDtypeStruct((N_MAIN, n, BRANCH_W), F32), kv_shape, kv_shape],
        scratch_shapes=[pltpu.VMEM((tm, D_MODEL), BF16)],
        input_output_aliases={3: 1, 4: 2},
        compiler_params=_cparams("parallel", "arbitrary"),
    )(x2d, w_all, w_kv_t, *kv_bufs)
    return main, (k_buf, v_buf)


def _pool_mix(acc, cnt, u, z, w_ref, sc_ref, g):
    sl = slice(g * POOL_GW, (g + 1) * POOL_GW)
    pooled = acc / cnt - u
    mixed = jnp.dot(pooled.astype(BF16), w_ref[g], preferred_element_type=F32)
    return mixed * sc_ref[:, sl] * _silu(z)


def _pool_prompt_kernel(u_ref, halo_ref, z_ref, w_ref, sc_ref, o_ref, ext_ref, *, tm):
    i = pl.program_id(1)
    ext_ref[0:POOL_HALO, :] = jnp.where(i == 0, 0.0, halo_ref[...])
    ext_ref[POOL_HALO:POOL_HALO + tm, :] = u_ref[...]
    pos = i * tm + lax.broadcasted_iota(jnp.int32, (tm, 1), 0)
    for g, w in enumerate(POOL_WINDOWS):
        sl = slice(g * POOL_GW, (g + 1) * POOL_GW)
        acc = ext_ref[POOL_HALO:POOL_HALO + tm, sl]
        for k in range(1, w):
            acc = acc + ext_ref[POOL_HALO - k:POOL_HALO - k + tm, sl]
        cnt = jnp.minimum(pos + 1, w).astype(F32)
        o_ref[:, sl] = _pool_mix(acc, cnt, u_ref[:, sl], z_ref[:, sl], w_ref, sc_ref, g)


def _pool_prompt(proj4, pool_w_bf16, pool_scale, tm):
    _, b, t, _ = proj4.shape
    hb = tm // POOL_HALO
    return pl.pallas_call(
        functools.partial(_pool_prompt_kernel, tm=tm),
        grid=(b, t // tm),
        in_specs=[pl.BlockSpec((None, None, tm, BRANCH_W), lambda bb, i: (P_UA, bb, i, 0)),
                  pl.BlockSpec((None, None, POOL_HALO, BRANCH_W),
                               lambda bb, i: (P_UA, bb, jnp.maximum(i * hb - 1, 0), 0)),
                  pl.BlockSpec((None, None, tm, BRANCH_W), lambda bb, i: (P_ZA, bb, i, 0)),
                  pl.BlockSpec((len(POOL_WINDOWS), POOL_GW, POOL_GW), lambda bb, i: (0, 0, 0)),
                  pl.BlockSpec((1, BRANCH_W), lambda bb, i: (0, 0))],
        out_specs=pl.BlockSpec((None, tm, BRANCH_W), lambda bb, i: (bb, i, 0)),
        out_shape=jax.ShapeDtypeStruct((b, t, BRANCH_W), F32),
        scratch_shapes=[pltpu.VMEM((POOL_HALO + tm, BRANCH_W), F32)],
        compiler_params=_cparams("parallel", "parallel"),
    )(proj4, proj4, proj4, pool_w_bf16, pool_scale)


def _pool_sample_kernel(u_ref, prev_ref, z_ref, w_ref, sc_ref, o_ref, *, pos0):
    for g, w in enumerate(POOL_WINDOWS):
        sl = slice(g * POOL_GW, (g + 1) * POOL_GW)
        acc = u_ref[:, sl]
        for k in range(1, w):
            acc = acc + prev_ref[POOL_BUF - k, :, sl]
        cnt = float(min(pos0 + 1, w))
        o_ref[:, sl] = _pool_mix(acc, cnt, u_ref[:, sl], z_ref[:, sl], w_ref, sc_ref, g)


def _pool_sample(proj_s, prev_t, pool_w_bf16, pool_scale, pos0):
    s = proj_s.shape[1]
    return pl.pallas_call(
        functools.partial(_pool_sample_kernel, pos0=pos0),
        grid=(1,),
        in_specs=[pl.BlockSpec((None, s, BRANCH_W), lambda i: (P_UA, 0, 0)),
                  pl.BlockSpec((POOL_BUF, s, BRANCH_W), lambda i: (0, 0, 0)),
                  pl.BlockSpec((None, s, BRANCH_W), lambda i: (P_ZA, 0, 0)),
                  pl.BlockSpec((len(POOL_WINDOWS), POOL_GW, POOL_GW), lambda i: (0, 0, 0)),
                  pl.BlockSpec((1, BRANCH_W), lambda i: (0, 0))],
        out_specs=pl.BlockSpec((s, BRANCH_W), lambda i: (0, 0)),
        out_shape=jax.ShapeDtypeStruct((s, BRANCH_W), F32),
        compiler_params=_cparams("arbitrary"),
    )(proj_s, prev_t, proj_s, pool_w_bf16, pool_scale)


def _s5_discretize(a_re, a_im, b_re, b_im, c_re, c_im, log_dt):
    lam = lax.complex(a_re, a_im)
    step = lam * jnp.exp(log_dt)[:, None]
    a_bar = jnp.exp(step)
    b_bar = ((a_bar - 1.0) / lam)[..., None] * lax.complex(b_re, b_im)
    c_mat = lax.complex(c_re, c_im)
    return step, a_bar, b_bar, c_mat


def _same_group(rows, row_unit, cols, col_unit, n_groups):
    r = lax.broadcasted_iota(jnp.int32, (rows, cols), 0) // row_unit % n_groups
    c = lax.broadcasted_iota(jnp.int32, (rows, cols), 1) // col_unit % n_groups
    return r == c


def _spread_tokens(m):
    k, n = S5_CHUNK * S5_GROUP, S5_CHUNK_W
    r = lax.broadcasted_iota(jnp.int32, (k, n), 0)
    c = lax.broadcasted_iota(jnp.int32, (k, n), 1)
    sel = ((r // S5_GROUP == c // S5_SLAB_W) & (r % S5_GROUP == c % S5_GROUP)).astype(F32)
    return jnp.einsum('...rk,kc->...rc', m, sel, precision=HIGHEST)


def _s5_chunk_operators(a_re, a_im, b_re, b_im, c_re, c_im, log_dt):
    ell = S5_CHUNK
    step, _, b_bar, c_mat = _s5_discretize(a_re, a_im, b_re, b_im, c_re, c_im, log_dt)
    apow = jnp.exp(step[None] * jnp.arange(ell + 1, dtype=F32)[:, None, None])
    ca = c_mat[None] * apow[:, :, None, :]
    ca_re, ca_im = jnp.real(ca), jnp.imag(ca)
    bb_re, bb_im = jnp.real(b_bar), jnp.imag(b_bar)
    kmat = (jnp.einsum('jgop,gpi->gjoi', ca_re[:ell], bb_re, precision=HIGHEST)
            - jnp.einsum('jgop,gpi->gjoi', ca_im[:ell], bb_im, precision=HIGHEST))
    s_idx = jnp.arange(ell)[:, None]
    t_idx = jnp.arange(ell)[None, :]
    lag = jnp.clip(t_idx - s_idx, 0, ell - 1)
    tfull = jnp.where((t_idx >= s_idx)[None, :, :, None, None], kmat[:, lag], 0.0)
    ns, ng, sw = S5_SLABS, S5_SLAB_G, S5_SLAB_G * S5_STATE
    t_c = tfull.reshape(ns, ng, ell, ell, S5_GROUP, S5_GROUP).transpose(0, 2, 1, 5, 3, 4)
    t_c = t_c.reshape(ns, S5_CHUNK_W, ell * S5_GROUP)
    t_op = jnp.where(_same_group(S5_CHUNK_W, S5_GROUP, S5_CHUNK_W, S5_GROUP, ng), _spread_tokens(t_c), 0.0)
    w_c = apow[ell - 1 - jnp.arange(ell)][:, :, :, None] * b_bar[None]
    w_c = w_c.reshape(ell, ns, ng, S5_STATE, S5_GROUP).transpose(1, 0, 2, 4, 3)
    w_c = w_c.reshape(ns, S5_CHUNK_W, S5_STATE)
    w_mask = _same_group(S5_CHUNK_W, S5_GROUP, sw, S5_STATE, ng)
    w_half = lambda m: jnp.where(w_mask, jnp.tile(m, (1, 1, ng)), 0.0)
    w_op = jnp.concatenate([w_half(jnp.real(w_c)), w_half(jnp.imag(w_c))], axis=2)
    v_mask = _same_group(sw, S5_STATE, S5_CHUNK_W, S5_GROUP, ng)

    def v_half(m):
        m = m.transpose(1, 3, 0, 2).reshape(ns, sw, ell * S5_GROUP)
        return jnp.where(v_mask, _spread_tokens(m), 0.0)

    v_op = jnp.concatenate([v_half(ca_re[1:]), v_half(-ca_im[1:])], axis=1)
    a_l = apow[ell]
    a_r = jnp.real(a_l).reshape(1, S5_GROUPS * S5_STATE)
    a_i = jnp.imag(a_l).reshape(1, S5_GROUPS * S5_STATE)
    return t_op.astype(BF16), w_op.astype(BF16), v_op.astype(BF16), a_r, a_i


def _s5_gather_chunks(u_ref, tr):
    rows = [u_ref[pl.ds(s, tr, stride=S5_CHUNK), :] for s in range(S5_CHUNK)]
    return jnp.concatenate(rows, axis=1).astype(BF16)


def _s5_state_kernel(u_ref, w_ref, o_ref, *, tr):
    o_ref[...] = jnp.dot(_s5_gather_chunks(u_ref, tr), w_ref[...], preferred_element_type=F32)


def _s5_chunk_states(proj, w_op, tr):
    n = proj.shape[1]
    r = n // S5_CHUNK
    return pl.pallas_call(
        functools.partial(_s5_state_kernel, tr=tr),
        grid=(S5_SLABS, r // tr),
        in_specs=[pl.BlockSpec((None, tr * S5_CHUNK, S5_SLAB_W), lambda g, i: (P_UB, i, g)),
                  pl.BlockSpec((None, S5_CHUNK_W, S5_CHUNK_W), lambda g, i: (g, 0, 0))],
        out_specs=pl.BlockSpec((tr, S5_CHUNK_W), lambda g, i: (i, g)),
        out_shape=jax.ShapeDtypeStruct((r, S5_SLABS * S5_CHUNK_W), F32),
        compiler_params=_cparams("parallel", "parallel"),
    )(proj, w_op)


def _s5_scan_kernel(s_ref, ar_ref, ai_ref, hprev_ref, hr_ref, hi_ref, hr_scr, hi_scr, *, cb):
    j = pl.program_id(1)
    sw = S5_SLAB_G * S5_STATE

    @pl.when(j == 0)
    def _():
        hr_scr[...] = jnp.zeros_like(hr_scr)
        hi_scr[...] = jnp.zeros_like(hi_scr)

    for n in range(S5_SLABS):
        re = slice(n * 2 * sw, n * 2 * sw + sw)
        im = slice(n * 2 * sw + sw, (n + 1) * 2 * sw)
        st = slice(n * sw, (n + 1) * sw)
        a_r = ar_ref[:, st]
        a_i = ai_ref[:, st]

        def body(c, carry, re=re, im=im, a_r=a_r, a_i=a_i):
            h_r, h_i = carry
            row = pl.ds(c, 1)
            hprev_ref[row, re] = h_r
            hprev_ref[row, im] = h_i
            n_r = a_r * h_r - a_i * h_i + s_ref[row, re]
            n_i = a_r * h_i + a_i * h_r + s_ref[row, im]
            return n_r, n_i

        h_r, h_i = lax.fori_loop(0, cb, body, (hr_scr[:, st], hi_scr[:, st]))
        hr_scr[:, st] = h_r
        hi_scr[:, st] = h_i

    @pl.when(j == pl.num_programs(1) - 1)
    def _():
        hr_ref[...] = hr_scr[...]
        hi_ref[...] = hi_scr[...]


def _s5_scan(s_loc, a_r, a_i, bsz, cb):
    r, w = s_loc.shape
    nj = r // bsz // cb
    sw = a_r.shape[1]
    vec = pl.BlockSpec((1, sw), lambda bb, j: (0, 0))
    last = pl.BlockSpec((None, 1, sw), lambda bb, j: (bb, 0, 0))
    blk = pl.BlockSpec((cb, w), lambda bb, j: (bb * nj + j, 0))
    return pl.pallas_call(
        functools.partial(_s5_scan_kernel, cb=cb),
        grid=(bsz, nj),
        in_specs=[blk, vec, vec],
        out_specs=[blk, last, last],
        out_shape=[jax.ShapeDtypeStruct((r, w), F32),
                   jax.ShapeDtypeStruct((bsz, 1, sw), F32),
                   jax.ShapeDtypeStruct((bsz, 1, sw), F32)],
        scratch_shapes=[pltpu.VMEM((1, sw), F32), pltpu.VMEM((1, sw), F32)],
        compiler_params=_cparams("parallel", "arbitrary"),
    )(s_loc, a_r, a_i)


def _s5_out_kernel(u_ref, h_ref, t_ref, v_ref, y_ref, *, tr):
    y = (jnp.dot(_s5_gather_chunks(u_ref, tr), t_ref[...], preferred_element_type=F32)
         + jnp.dot(h_ref[...].astype(BF16), v_ref[...], preferred_element_type=F32))
    for s in range(S5_CHUNK):
        y_ref[pl.ds(s, tr, stride=S5_CHUNK), :] = y[:, s * S5_SLAB_W:(s + 1) * S5_SLAB_W]


def _s5_chunk_outputs(proj, h_prev, t_op, v_op, tr):
    n = proj.shape[1]
    r = n // S5_CHUNK
    op_spec = pl.BlockSpec((None, S5_CHUNK_W, S5_CHUNK_W), lambda g, i: (g, 0, 0))
    return pl.pallas_call(
        functools.partial(_s5_out_kernel, tr=tr),
        grid=(S5_SLABS, r // tr),
        in_specs=[pl.BlockSpec((None, tr * S5_CHUNK, S5_SLAB_W), lambda g, i: (P_UB, i, g)),
                  pl.BlockSpec((tr, S5_CHUNK_W), lambda g, i: (i, g)),
                  op_spec, op_spec],
        out_specs=pl.BlockSpec((tr * S5_CHUNK, S5_SLAB_W), lambda g, i: (i, g)),
        out_shape=jax.ShapeDtypeStruct((n, BRANCH_W), F32),
        compiler_params=_cparams("parallel", "parallel"),
    )(proj, h_prev, t_op, v_op)


def _s5_prompt(proj, ops, bsz):
    t_op, w_op, v_op, a_r, a_i = ops
    r = proj.shape[1] // S5_CHUNK
    tr = _tile(r, 256)
    s_loc = _s5_chunk_states(proj, w_op, tr)
    h_prev, h_re, h_im = _s5_scan(s_loc, a_r, a_i, bsz, _tile(r // bsz, 128))
    return _s5_chunk_outputs(proj, h_prev, t_op, v_op, tr), h_re, h_im


def _s5_sample_operators(a_re, a_im, b_re, b_im, c_re, c_im, log_dt):
    _, a_bar, b_bar, c_mat = _s5_discretize(a_re, a_im, b_re, b_im, c_re, c_im, log_dt)
    ns = BRANCH_W // 256
    gs = S5_GROUPS // ns

    def expand_b(m):
        m = m.reshape(ns, gs, S5_STATE, S5_GROUP).transpose(0, 1, 3, 2)
        m = jnp.tile(m.reshape(ns, gs * S5_GROUP, S5_STATE), (1, 1, gs))
        return jnp.where(_same_group(m.shape[1], S5_GROUP, m.shape[2], S5_STATE, gs), m, 0.0)

    def expand_c(m):
        m = m.reshape(ns, gs, S5_GROUP, S5_STATE).transpose(0, 1, 3, 2)
        m = jnp.tile(m.reshape(ns, gs * S5_STATE, S5_GROUP), (1, 1, gs))
        return jnp.where(_same_group(m.shape[1], S5_STATE, m.shape[2], S5_GROUP, gs), m, 0.0)

    b_op = jnp.concatenate([expand_b(jnp.real(b_bar)), expand_b(jnp.imag(b_bar))], axis=2)
    c_op = jnp.concatenate([expand_c(jnp.real(c_mat)), expand_c(-jnp.imag(c_mat))], axis=1)
    a_r = jnp.real(a_bar).reshape(1, S5_GROUPS * S5_STATE)
    a_i = jnp.imag(a_bar).reshape(1, S5_GROUPS * S5_STATE)
    return b_op.astype(BF16), c_op.astype(BF16), a_r, a_i


def _s5_sample_kernel(u_ref, hr_ref, hi_ref, ar_ref, ai_ref, b_ref, c_ref, y_ref, or_ref, oi_ref):
    ns = b_ref.shape[0]
    cw = b_ref.shape[1]
    sw = c_ref.shape[1] // 2
    for n in range(ns):
        cs = slice(n * cw, (n + 1) * cw)
        ss = slice(n * sw, (n + 1) * sw)
        bu = jnp.dot(u_ref[:, cs].astype(BF16), b_ref[n], preferred_element_type=F32)
        a_r, a_i = ar_ref[:, ss], ai_ref[:, ss]
        h_r, h_i = hr_ref[:, ss], hi_ref[:, ss]
        n_r = a_r * h_r - a_i * h_i + bu[:, :sw]
        n_i = a_r * h_i + a_i * h_r + bu[:, sw:]
        or_ref[:, ss] = n_r
        oi_ref[:, ss] = n_i
        hcat = jnp.concatenate([n_r, n_i], axis=1).astype(BF16)
        y_ref[:, cs] = jnp.dot(hcat, c_ref[n], preferred_element_type=F32)


def _s5_sample(proj_s, h_re, h_im, ops):
    b_op, c_op, a_r, a_i = ops
    s = proj_s.shape[1]
    sw = h_re.shape[1]
    full = lambda shape: pl.BlockSpec(shape, lambda i: (0,) * len(shape))
    return pl.pallas_call(
        _s5_sample_kernel,
        grid=(1,),
        in_specs=[pl.BlockSpec((None, s, BRANCH_W), lambda i: (P_UB, 0, 0)),
                  full((s, sw)), full((s, sw)), full((1, sw)), full((1, sw)),
                  full(b_op.shape), full(c_op.shape)],
        out_specs=[full((s, BRANCH_W)), full((s, sw)), full((s, sw))],
        out_shape=[jax.ShapeDtypeStruct((s, BRANCH_W), F32),
                   jax.ShapeDtypeStruct((s, sw), F32),
                   jax.ShapeDtypeStruct((s, sw), F32)],
        compiler_params=_cparams("arbitrary"),
    )(proj_s, h_re, h_im, a_r, a_i, b_op, c_op)


def _glu_kernel(y_ref, u_ref, z_ref, d_ref, w_ref, b_ref, o_ref):
    y = y_ref[...] + d_ref[...] * u_ref[...]
    g = jax.nn.gelu(y, approximate=True)
    t = jnp.dot(g.astype(BF16), w_ref[...], preferred_element_type=F32) + b_ref[...]
    o_ref[...] = g * jax.nn.sigmoid(t) * _silu(z_ref[...])


def _glu(y2d, proj, s5_d, w_glu_bf16, b_glu, tm):
    n = y2d.shape[0]
    return pl.pallas_call(
        _glu_kernel,
        grid=(n // tm,),
        in_specs=[pl.BlockSpec((tm, BRANCH_W), lambda i: (i, 0)),
                  pl.BlockSpec((None, tm, BRANCH_W), lambda i: (P_UB, i, 0)),
                  pl.BlockSpec((None, tm, BRANCH_W), lambda i: (P_ZB, i, 0)),
                  pl.BlockSpec((1, BRANCH_W), lambda i: (0, 0)),
                  pl.BlockSpec((BRANCH_W, BRANCH_W), lambda i: (0, 0)),
                  pl.BlockSpec((1, BRANCH_W), lambda i: (0, 0))],
        out_specs=pl.BlockSpec((tm, BRANCH_W), lambda i: (i, 0)),
        out_shape=jax.ShapeDtypeStruct((n, BRANCH_W), F32),
        compiler_params=_cparams("parallel"),
    )(y2d, proj, proj, s5_d, w_glu_bf16, b_glu)


def _neg_suffix_ones(n):
    r = lax.broadcasted_iota(jnp.int32, (n, n), 0)
    c = lax.broadcasted_iota(jnp.int32, (n, n), 1)
    return jnp.where(r >= c, -1.0, 0.0).astype(BF16)


def _attn_kernel(bias_ref, q_ref, kt_ref, vt_ref, z_ref, m_ref, o_ref,
                 kb_scr, vb_scr, qm_scr, carry_scr, acc_scr, *, bq, bk):
    hp = pl.program_id(1)
    i = pl.program_id(2)
    lane = lax.broadcasted_iota(jnp.int32, (bq, PAIR_LANES), 1)

    own_lane = (lane < SB_HEAD_DIM, lane >= SB_HEAD_DIM)
    first_other = (SB_HEAD_DIM, 0)

    @pl.when(i == 0)
    def _():
        kt = kt_ref[...]
        row = lax.broadcasted_iota(jnp.int32, kt.shape, 0)
        for h in range(2):
            own = (row < SB_HEAD_DIM) if h == 0 else (row >= SB_HEAD_DIM)
            ones = (row >= first_other[h]) & (row < first_other[h] + 3)
            kb_scr[h] = jnp.where(own, kt, jnp.where(ones, 1.0, 0.0)).astype(BF16)
        vb_scr[...] = vt_ref[...].astype(BF16)

    q = q_ref[...] * SB_SCALE
    for h in range(2):
        rest = jnp.full((bq, PAIR_LANES), bias_ref[2 * hp + h], F32)
        qh = jnp.where(own_lane[h], q, 0.0)
        for t in range(3):
            term = rest.astype(BF16).astype(F32)
            qh = jnp.where(lane == first_other[h] + t, term, qh)
            rest = rest - term
        qm_scr[h] = qh.astype(BF16)
    carry_scr[...] = jnp.zeros_like(carry_scr)
    acc_scr[...] = jnp.zeros_like(acc_scr)

    def tile(j, r0, rows, masked):
        rs = slice(r0, r0 + rows)
        ks = pl.multiple_of(j * bk, bk)
        vb = vb_scr[:, pl.ds(ks, bk)]
        if masked:
            qpos = i * bq + r0 + lax.broadcasted_iota(jnp.int32, (rows, bk), 0)
            kpos = j * bk + lax.broadcasted_iota(jnp.int32, (rows, bk), 1)
            causal = kpos < qpos
        for h in range(2):
            z = jnp.dot(qm_scr[h, rs, :], kb_scr[h, :, pl.ds(ks, bk)], preferred_element_type=F32)
            sp = _softplus(z)
            if masked:
                sp = jnp.where(causal, sp, 0.0)
            nsuf = jnp.dot(sp.astype(BF16), m_ref[...], preferred_element_type=F32)
            carry = carry_scr[h, rs, :]
            expo = (z + nsuf) + jnp.concatenate([carry] * (bk // PAIR_LANES), axis=1)
            w = jnp.exp2(expo * LOG2E)
            if masked:
                w = jnp.where(causal, w, 0.0)
            acc_scr[h, rs, :] += lax.dot_general(w.astype(BF16), vb, (((1,), (1,)), ((), ())),
                                                 preferred_element_type=F32)
            carry_scr[h, rs, :] = carry + jnp.broadcast_to(nsuf[:, 0:1], (rows, PAIR_LANES))

    r = bq // bk
    for jl in reversed(range(r)):
        tile(r * i + jl, jl * bk, bq - jl * bk, True)

    def body(t, c):
        tile(r * i - 1 - t, 0, bq, False)
        return c

    lax.fori_loop(0, r * i, body, 0)

    y = jnp.where(lane < SB_HEAD_DIM, acc_scr[0], acc_scr[1])
    o_ref[...] = y * _silu(z_ref[...])


def _attention_prompt(proj4, kv_bufs, sb_bias, layer, bq, bk):
    _, b, t, _ = proj4.shape
    kv_spec = pl.BlockSpec((None, None, PAIR_LANES, t), lambda bb, hp, i, s: (layer, bb, hp, 0))
    grid_spec = pltpu.PrefetchScalarGridSpec(
        num_scalar_prefetch=1,
        grid=(b, HEAD_PAIRS, t // bq),
        in_specs=[pl.BlockSpec((None, None, bq, PAIR_LANES), lambda bb, hp, i, s: (P_Q, bb, i, hp)),
                  kv_spec, kv_spec,
                  pl.BlockSpec((None, None, bq, PAIR_LANES), lambda bb, hp, i, s: (P_ZC, bb, i, hp)),
                  pl.BlockSpec((bk, bk), lambda bb, hp, i, s: (0, 0))],
        out_specs=pl.BlockSpec((None, bq, PAIR_LANES), lambda bb, hp, i, s: (bb, i, hp)),
        scratch_shapes=[pltpu.VMEM((2, PAIR_LANES, t), BF16), pltpu.VMEM((PAIR_LANES, t), BF16),
                        pltpu.VMEM((2, bq, PAIR_LANES), BF16),
                        pltpu.VMEM((2, bq, PAIR_LANES), F32), pltpu.VMEM((2, bq, PAIR_LANES), F32)])
    return pl.pallas_call(
        functools.partial(_attn_kernel, bq=bq, bk=bk),
        grid_spec=grid_spec,
        out_shape=jax.ShapeDtypeStruct((b, t, BRANCH_W), F32),
        compiler_params=_cparams("parallel", "parallel", "arbitrary"),
    )(sb_bias, proj4, kv_bufs[0], kv_bufs[1], proj4, _neg_suffix_ones(bk))


def _decode_kernel(pt_ref, qt_ref, z_ref, *refs, pps):
    k_refs = refs[:pps]
    v_refs = refs[pps:2 * pps]
    bias_ref, m_ref, o_ref, qb_scr, carry_scr, acc_scr = refs[2 * pps:]
    s = pl.program_id(1)
    page = k_refs[0].shape[2]

    @pl.when(s == 0)
    def _():
        qt = qt_ref[...] * SB_SCALE
        for h in range(SB_HEADS):
            qb_scr[h] = jnp.broadcast_to(qt[:, h:h + 1], (SB_HEAD_DIM, page))
        carry_scr[...] = jnp.zeros_like(carry_scr)
        acc_scr[...] = jnp.zeros_like(acc_scr)

    z = jnp.concatenate([jnp.sum(k_refs[p][h] * qb_scr[h], axis=0, keepdims=True)
                         for p in range(pps) for h in range(SB_HEADS)], axis=0) + bias_ref[...]
    sp = _softplus(z)
    hi = sp.astype(BF16)
    lo = (sp - hi.astype(F32)).astype(BF16)
    nsuf = (jnp.dot(hi, m_ref[...], preferred_element_type=F32)
            + jnp.dot(lo, m_ref[...], preferred_element_type=F32))
    carry = carry_scr[...]
    ws = []
    for p in range(pps):
        rows = slice(p * SB_HEADS, (p + 1) * SB_HEADS)
        ws.append(jnp.exp2(((z[rows] + nsuf[rows]) + carry) * LOG2E))
        carry = carry + jnp.broadcast_to(nsuf[rows][:, 0:1], (SB_HEADS, page))
    carry_scr[...] = carry
    for h in range(SB_HEADS):
        acc = acc_scr[h]
        for p in range(pps):
            acc = acc + v_refs[p][h] * jnp.broadcast_to(ws[p][h:h + 1, :], (SB_HEAD_DIM, page))
        acc_scr[h] = acc

    @pl.when(s == pl.num_programs(1) - 1)
    def _():
        y = jnp.sum(acc_scr[...], axis=-1)
        o_ref[...] = y * _silu(z_ref[...])


def _attention_sample(proj_s, cache_k, cache_v, page_table, bias_col, layer):
    s = proj_s.shape[1]
    n_pages = page_table.shape[1]
    page = cache_k.shape[2]
    pps = next(c for c in (16, 8, 4, 2, 1) if n_pages % c == 0)
    ck = cache_k.transpose(0, 1, 3, 4, 2)
    cv = cache_v.transpose(0, 1, 3, 4, 2)
    heads = lambda part: proj_s[part].reshape(s, SB_HEADS, SB_HEAD_DIM)
    q_t = heads(P_Q).transpose(0, 2, 1)

    def page_spec(j):
        return pl.BlockSpec((None, None, SB_HEADS, SB_HEAD_DIM, page),
                            lambda bb, st, pt: (layer, pt[bb, n_pages - 1 - (st * pps + j)], 0, 0, 0))

    grid_spec = pltpu.PrefetchScalarGridSpec(
        num_scalar_prefetch=1,
        grid=(s, n_pages // pps),
        in_specs=[pl.BlockSpec((None, SB_HEAD_DIM, SB_HEADS), lambda bb, st, pt: (bb, 0, 0)),
                  pl.BlockSpec((None, SB_HEADS, SB_HEAD_DIM), lambda bb, st, pt: (bb, 0, 0))]
                 + [page_spec(j) for j in range(pps)] * 2
                 + [pl.BlockSpec((pps * SB_HEADS, 1), lambda bb, st, pt: (0, 0)),
                    pl.BlockSpec((page, page), lambda bb, st, pt: (0, 0))],
        out_specs=pl.BlockSpec((None, SB_HEADS, SB_HEAD_DIM), lambda bb, st, pt: (bb, 0, 0)),
        scratch_shapes=[pltpu.VMEM((SB_HEADS, SB_HEAD_DIM, page), F32),
                        pltpu.VMEM((SB_HEADS, page), F32),
                        pltpu.VMEM((SB_HEADS, SB_HEAD_DIM, page), F32)])
    out = pl.pallas_call(
        functools.partial(_decode_kernel, pps=pps),
        grid_spec=grid_spec,
        out_shape=jax.ShapeDtypeStruct((s, SB_HEADS, SB_HEAD_DIM), F32),
        compiler_params=_cparams("parallel", "arbitrary"),
    )(page_table, q_t, heads(P_ZC), *([ck] * pps), *([cv] * pps), jnp.tile(bias_col, (pps, 1)), _neg_suffix_ones(page))
    return out.reshape(s, BRANCH_W)


def _merge_kernel(ya_ref, yb_ref, yc_ref, ga_ref, gb_ref, gc_ref, x_ref, wb_ref, wo_ref, g_ref, b_ref, o_ref,
                  *, alpha):
    merged = None
    for n, (y_ref, gate_ref) in enumerate(((ya_ref, ga_ref), (yb_ref, gb_ref), (yc_ref, gc_ref))):
        pb = jnp.dot(y_ref[...].astype(BF16), wb_ref[n], preferred_element_type=F32)
        term = jax.nn.sigmoid(gate_ref[...]) * pb
        merged = term if merged is None else merged + term
    out = jnp.dot(merged.astype(BF16), wo_ref[...], preferred_element_type=F32)
    v = alpha * x_ref[...] + out
    mu = jnp.mean(v, axis=-1, keepdims=True)
    c = v - mu
    var = jnp.mean(c * c, axis=-1, keepdims=True)
    o_ref[...] = c * lax.rsqrt(var + LN_EPS) * g_ref[...] + b_ref[...]


def _merge(ya, yb, yc, proj, x2d, w_branch_bf16, w_out_bf16, ln_g, ln_b, alpha, tm):
    n = x2d.shape[0]
    row = pl.BlockSpec((tm, D_MODEL), lambda i: (i, 0))
    part = lambda k: pl.BlockSpec((None, tm, D_MODEL), lambda i: (k, i, 0))
    vec = pl.BlockSpec((1, D_MODEL), lambda i: (0, 0))
    return pl.pallas_call(
        functools.partial(_merge_kernel, alpha=alpha),
        grid=(n // tm,),
        in_specs=[row, row, row, part(P_GA), part(P_GB), part(P_GC), row,
                  pl.BlockSpec((3, BRANCH_W, D_MODEL), lambda i: (0, 0, 0)),
                  pl.BlockSpec((D_MODEL, D_MODEL), lambda i: (0, 0)),
                  vec, vec],
        out_specs=row,
        out_shape=jax.ShapeDtypeStruct((n, D_MODEL), F32),
        compiler_params=_cparams("parallel"),
    )(ya, yb, yc, proj, proj, proj, x2d, w_branch_bf16, w_out_bf16, ln_g, ln_b)


def _tile(n, pref):
    return pref if n % pref == 0 else n


def _kv_weights_t(w_in):
    lo, hi = KV_PARTS[0] * BRANCH_W, (KV_PARTS[-1] + 1) * BRANCH_W
    w_kv = w_in[:, :, lo:hi].reshape(w_in.shape[0], D_MODEL, len(KV_PARTS), BRANCH_W)
    return w_kv.transpose(0, 2, 3, 1).astype(BF16)


def _heads_last(kv_t):
    d, b, _, t = kv_t.shape
    return kv_t.reshape(d, b, SB_HEADS, SB_HEAD_DIM, t).transpose(0, 1, 4, 2, 3)


def kernel(x_prompt, x_sample, cache_k, cache_v, state_pool, state_s5_re, state_s5_im, page_table, w_in, pool_w, pool_scale, s5_a_re, s5_a_im, s5_b_re, s5_b_im, s5_c_re, s5_c_im, s5_d, s5_log_dt, w_glu, b_glu, sb_bias, w_branch, w_out, ln_g, ln_b):
    depth = w_in.shape[0]
    alpha = (2.0 * depth) ** 0.25
    bsz, t_len, _ = x_prompt.shape
    dec_b = x_sample.shape[0]
    past_len = page_table.shape[1] * cache_k.shape[2]
    n_prompt = bsz * t_len

    xp = x_prompt.reshape(n_prompt, D_MODEL)
    xs = x_sample.reshape(dec_b, D_MODEL)
    kv_p = (jnp.zeros((depth, bsz, BRANCH_W, t_len), F32),) * 2
    kv_s = (jnp.zeros((depth, 1, BRANCH_W, dec_b), F32),) * 2
    outs = [[] for _ in range(6)]
    w_in_all, w_kv_t_all = w_in.astype(BF16), _kv_weights_t(w_in)
    pool_w_all, w_glu_all = pool_w.astype(BF16), w_glu.astype(BF16)
    w_branch_all, w_out_all = w_branch.astype(BF16), w_out.astype(BF16)
    s5_params = (s5_a_re, s5_a_im, s5_b_re, s5_b_im, s5_c_re, s5_c_im, s5_log_dt)
    chunk_ops_all = jax.vmap(_s5_chunk_operators)(*s5_params)
    sample_ops_all = jax.vmap(_s5_sample_operators)(*s5_params)
    for l in range(depth):
        pool_w_b, w_glu_b, w_branch_b, w_out_b = pool_w_all[l], w_glu_all[l], w_branch_all[l], w_out_all[l]
        row = lambda a: a[l].reshape(1, -1)
        chunk_ops = tuple(o[l] for o in chunk_ops_all)
        sample_ops = tuple(o[l] for o in sample_ops_all)

        proj, kv_p = _project(xp, w_in_all, w_kv_t_all, kv_p, l, bsz, t_len, _tile(t_len, 1024))
        proj4 = proj.reshape(N_MAIN, bsz, t_len, BRANCH_W)
        ya = _pool_prompt(proj4, pool_w_b, row(pool_scale), _tile(t_len, 512)).reshape(n_prompt, BRANCH_W)
        y5, hp_re, hp_im = _s5_prompt(proj, chunk_ops, bsz)
        yb = _glu(y5, proj, row(s5_d), w_glu_b, row(b_glu), _tile(n_prompt, 512))
        yc = _attention_prompt(proj4, kv_p, sb_bias[l], l, _tile(t_len, 2048), 256).reshape(n_prompt, BRANCH_W)
        xp = _merge(ya, yb, yc, proj, xp, w_branch_b, w_out_b, row(ln_g), row(ln_b), alpha, _tile(n_prompt, 256))
        outs[0].append(proj4[P_UA, :, t_len - POOL_BUF:, :])
        outs[1].append(hp_re.reshape(bsz, S5_GROUPS, S5_STATE))
        outs[2].append(hp_im.reshape(bsz, S5_GROUPS, S5_STATE))

        proj_s, kv_s = _project(xs, w_in_all, w_kv_t_all, kv_s, l, 1, dec_b, dec_b)
        prev_t = state_pool[l].transpose(1, 0, 2)
        ya_s = _pool_sample(proj_s, prev_t, pool_w_b, row(pool_scale), past_len)
        h_re = state_s5_re[l].reshape(dec_b, S5_GROUPS * S5_STATE)
        h_im = state_s5_im[l].reshape(dec_b, S5_GROUPS * S5_STATE)
        y5_s, n_re, n_im = _s5_sample(proj_s, h_re, h_im, sample_ops)
        yb_s = _glu(y5_s, proj_s, row(s5_d), w_glu_b, row(b_glu), dec_b)
        yc_s = _attention_sample(proj_s, cache_k, cache_v, page_table, sb_bias[l].reshape(SB_HEADS, 1), l)
        xs = _merge(ya_s, yb_s, yc_s, proj_s, xs, w_branch_b, w_out_b, row(ln_g), row(ln_b), alpha, dec_b)
        outs[3].append(jnp.concatenate([prev_t[1:], proj_s[P_UA][None]], axis=0).transpose(1, 0, 2))
        outs[4].append(n_re.reshape(dec_b, S5_GROUPS, S5_STATE))
        outs[5].append(n_im.reshape(dec_b, S5_GROUPS, S5_STATE))

    pool_p, s5r_p, s5i_p, pool_s, s5r_s, s5i_s = (jnp.stack(o) for o in outs)
    k_p, v_p = _heads_last(kv_p[0]), _heads_last(kv_p[1])
    k_s = _heads_last(kv_s[0]).transpose(0, 2, 1, 3, 4)
    v_s = _heads_last(kv_s[1]).transpose(0, 2, 1, 3, 4)
    return (xp.reshape(bsz, t_len, D_MODEL), xs.reshape(dec_b, 1, D_MODEL),
            k_p, v_p, pool_p, s5r_p, s5i_p, k_s, v_s, pool_s, s5r_s, s5i_s)
```
